```python
import jax, jax.numpy as jnp
from jax import lax
import numpy as np

D_MODEL = 1024
BATCH = 2
SEQ = 8192
DEPTH = 2
DEC_BATCH = 128
DEC_SEQ = 8
PAST_LEN = 2048
PAGE_SIZE = 128

HEAD_DIM = 128
HEADS_PER_GROUP = 4
ATTN_GROUPS = ((128, 1), (512, 4), (2048, 16))
N_GROUPS = len(ATTN_GROUPS)
N_HEADS = N_GROUPS * HEADS_PER_GROUP
ATTN_W = N_HEADS * HEAD_DIM
ATTN_OUT_W = HEADS_PER_GROUP * HEAD_DIM
LRU_W = D_MODEL
LRU_BLOCKS = 8
LRU_BS = LRU_W // LRU_BLOCKS
CONV_W = 4
LRU_C = 8.0
D_FF = 2816
ROPE_THETA = 10000.0
LN_EPS = 1e-5
Q_BLOCK = 128
ALPHA = (2 * DEPTH) ** 0.25
BETA = (8 * DEPTH) ** -0.25
SPLITS = [ATTN_W, 2 * ATTN_W, 3 * ATTN_W, 3 * ATTN_W + LRU_W, 3 * ATTN_W + LRU_W + D_MODEL]
IN_W = 3 * ATTN_W + LRU_W + 2 * D_MODEL

kernel_name = "hybrid_dilated_attn_rglru_macaron_step"


def layer_norm(x, g, b):
    xf = x.astype(jnp.float32)
    mu = jnp.mean(xf, -1, keepdims=True)
    var = jnp.mean(jnp.square(xf - mu), -1, keepdims=True)
    y = (xf - mu) * lax.rsqrt(var + LN_EPS) * g.astype(jnp.float32) + b.astype(jnp.float32)
    return y.astype(x.dtype)


def swiglu(x, w_in, w_out):
    g, u = jnp.split(x @ w_in, 2, axis=-1)
    return (jax.nn.silu(g) * u) @ w_out


def rope(x, pos):
    half = HEAD_DIM // 2
    inv = ROPE_THETA ** (-jnp.arange(half, dtype=jnp.float32) / half)
    ang = pos[:, None] * inv[None, :]
    cos = jnp.cos(ang)[None, :, None, :]
    sin = jnp.sin(ang)[None, :, None, :]
    xf = x.astype(jnp.float32)
    x1, x2 = xf[..., :half], xf[..., half:]
    return jnp.concatenate([x1 * cos - x2 * sin, x2 * cos + x1 * sin], -1).astype(x.dtype)


def dilated_group(q, kv, q_idx, window, dil):
    n_keys = window // dil + 1
    idx = q_idx[:, None] - dil * jnp.arange(n_keys, dtype=jnp.int32)[None, :]
    valid = idx >= 0
    kvg = kv[:, jnp.maximum(idx, 0)]
    kg, vg = kvg[:, :, :, 0], kvg[:, :, :, 1]
    s = jnp.einsum('bqhd,bqmhd->bqhm', q, kg).astype(jnp.float32)
    s = jnp.where(valid[None, :, None, :], s, -jnp.inf)
    lse = jax.nn.logsumexp(s, axis=-1)
    p = jnp.exp(s - lse[..., None])
    o = jnp.einsum('bqhm,bqmhd->bqhd', p.astype(vg.dtype), vg)
    return o, lse


def dilated_attention(q, kv_alls, offsets):
    B, Q = q.shape[0], q.shape[1]
    qb = Q_BLOCK if Q % Q_BLOCK == 0 else Q
    nb = Q // qb
    qs = (q * HEAD_DIM ** -0.5).reshape(B, nb, qb, N_GROUPS, HEADS_PER_GROUP, HEAD_DIM).transpose(1, 0, 2, 3, 4, 5)
    local = jnp.arange(Q, dtype=jnp.int32).reshape(nb, qb)

    def block(args):
        qblk, iblk = args
        outs, lses = [], []
        for g, (w, d) in enumerate(ATTN_GROUPS):
            o, l = dilated_group(qblk[:, :, g], kv_alls[g], iblk + offsets[g], w, d)
            outs.append(o)
            lses.append(l)
        wts = jax.nn.softmax(jnp.stack(lses, 0), axis=0)
        o = jnp.einsum('gbqh,gbqhd->bqhd', wts, jnp.stack(outs, 0).astype(jnp.float32))
        return o.astype(q.dtype)

    o = lax.map(block, (qs, local))
    return o.transpose(1, 0, 2, 3, 4).reshape(B, Q, ATTN_OUT_W)


def causal_conv(x, buf, w, b):
    L = x.shape[1]
    xp = jnp.concatenate([buf.astype(x.dtype), x], 1)
    y = b + sum(xp[:, j:j + L] * w[j] for j in range(CONV_W))
    return y, xp[:, -(CONV_W - 1):]


def rg_lru(x, h0, wa, ba, wx, bx, lam):
    B, L, C = x.shape
    xb = x.reshape(B, L, LRU_BLOCKS, LRU_BS)
    r = jax.nn.sigmoid((jnp.einsum('blnc,ncd->blnd', xb, wa).reshape(B, L, C) + ba).astype(jnp.float32))
    i = jax.nn.sigmoid((jnp.einsum('blnc,ncd->blnd', xb, wx).reshape(B, L, C) + bx).astype(jnp.float32))
    log_a = -LRU_C * r * jax.nn.softplus(-lam.astype(jnp.float32))
    a = jnp.exp(log_a)
    bt = jnp.sqrt(-jnp.expm1(2.0 * log_a)) * i * x.astype(jnp.float32)
    bt = bt.at[:, 0].add(a[:, 0] * h0.astype(jnp.float32))

    def comb(e1, e2):
        a1, b1 = e1
        a2, b2 = e2
        return a1 * a2, a2 * b1 + b2

    _, h = lax.associative_scan(comb, (a, bt), axis=1)
    return h.astype(x.dtype), h[:, -1].astype(x.dtype)


def mixer(x, pos0, kv_bufs, h0, conv_buf, w_in, conv_w, conv_b, lru_wa, lru_ba, lru_wx, lru_bx, lru_lambda,
          w_branch_a, w_branch_b, w_out):
    B, L, _ = x.shape
    q, k, v, xl, ga, gb = jnp.split(x @ w_in, SPLITS, axis=-1)
    pos = pos0 + jnp.arange(L, dtype=jnp.float32)
    q = rope(q.reshape(B, L, N_HEADS, HEAD_DIM), pos)
    k = rope(k.reshape(B, L, N_HEADS, HEAD_DIM), pos)
    v = v.reshape(B, L, N_HEADS, HEAD_DIM)
    kv_new, kv_alls, offsets = [], [], []
    for g in range(N_GROUPS):
        sl = slice(g * HEADS_PER_GROUP, (g + 1) * HEADS_PER_GROUP)
        kvn = jnp.stack([k[:, :, sl], v[:, :, sl]], axis=2)
        kv_new.append(kvn)
        kv_alls.append(jnp.concatenate([kv_bufs[g].astype(kvn.dtype), kvn], 1))
        offsets.append(kv_bufs[g].shape[1])
    o_a = dilated_attention(q, kv_alls, offsets)
    xc, conv_new = causal_conv(xl, conv_buf, conv_w, conv_b)
    o_b, h_new = rg_lru(xc, h0, lru_wa, lru_ba, lru_wx, lru_bx, lru_lambda)
    z = jax.nn.sigmoid(ga) * (o_a @ w_branch_a) + jax.nn.sigmoid(gb) * (o_b @ w_branch_b)
    return z @ w_out, kv_new, h_new, conv_new


def layer(x, pos0, kv_bufs, h0, conv_buf, ln_g, ln_b, f1i, f1o, f2i, f2o, w_in, conv_w, conv_b,
          lru_wa, lru_ba, lru_wx, lru_bx, lru_lambda, w_branch_a, w_branch_b, w_out):
    x = layer_norm(ALPHA * x + 0.5 * swiglu(x, f1i, f1o), ln_g[0], ln_b[0])
    m, kv_new, h_new, conv_new = mixer(x, pos0, kv_bufs, h0, conv_buf, w_in, conv_w, conv_b, lru_wa, lru_ba,
                                       lru_wx, lru_bx, lru_lambda, w_branch_a, w_branch_b, w_out)
    x = layer_norm(ALPHA * x + m, ln_g[1], ln_b[1])
    x = layer_norm(ALPHA * x + 0.5 * swiglu(x, f2i, f2o), ln_g[2], ln_b[2])
    return x, kv_new, h_new, conv_new


def setup_inputs(seed: int = 0) -> dict:
    key = jax.random.key(seed)
    ks = jax.random.split(key, 32)

    def nrm(k, shape, s):
        return jax.random.normal(k, shape, jnp.float32) * s

    x_prompt = nrm(ks[0], (BATCH, SEQ, D_MODEL), 1.0)
    x_sample = nrm(ks[1], (DEC_BATCH, DEC_SEQ, D_MODEL), 1.0)
    caches = [nrm(ks[2 + g], (DEPTH, DEC_BATCH, min(w, PAST_LEN), 2, HEADS_PER_GROUP, HEAD_DIM), 1.0)
              for g, (w, _) in enumerate(ATTN_GROUPS)]
    state_rglru_h = nrm(ks[5], (DEPTH, DEC_BATCH, LRU_W), 0.5)
    state_rglru_conv = nrm(ks[6], (DEPTH, DEC_BATCH, CONV_W - 1, LRU_W), 1.0)
    ln_g = 1.0 + nrm(ks[7], (DEPTH, 3, D_MODEL), 0.02)
    ln_b = nrm(ks[8], (DEPTH, 3, D_MODEL), 0.02)
    ffn1_w_in = nrm(ks[9], (DEPTH, D_MODEL, 2 * D_FF), BETA * D_MODEL ** -0.5)
    ffn1_w_out = nrm(ks[10], (DEPTH, D_FF, D_MODEL), BETA * D_FF ** -0.5)
    ffn2_w_in = nrm(ks[11], (DEPTH, D_MODEL, 2 * D_FF), BETA * D_MODEL ** -0.5)
    ffn2_w_out = nrm(ks[12], (DEPTH, D_FF, D_MODEL), BETA * D_FF ** -0.5)
    w_q = nrm(ks[13], (DEPTH, D_MODEL, ATTN_W), D_MODEL ** -0.5)
    w_k = nrm(ks[14], (DEPTH, D_MODEL, ATTN_W), D_MODEL ** -0.5)
    w_v = nrm(ks[15], (DEPTH, D_MODEL, ATTN_W), BETA * D_MODEL ** -0.5)
    w_x = nrm(ks[16], (DEPTH, D_MODEL, LRU_W), D_MODEL ** -0.5)
    w_g = nrm(ks[17], (DEPTH, D_MODEL, 2 * D_MODEL), D_MODEL ** -0.5)
    w_in = jnp.concatenate([w_q, w_k, w_v, w_x, w_g], axis=-1)
    conv_w = nrm(ks[18], (DEPTH, CONV_W, LRU_W), CONV_W ** -0.5)
    conv_b = nrm(ks[19], (DEPTH, LRU_W), 0.01)
    lru_wa = nrm(ks[20], (DEPTH, LRU_BLOCKS, LRU_BS, LRU_BS), LRU_BS ** -0.5)
    lru_ba = nrm(ks[21], (DEPTH, LRU_W), 0.01)
    lru_wx = nrm(ks[22], (DEPTH, LRU_BLOCKS, LRU_BS, LRU_BS), LRU_BS ** -0.5)
    lru_bx = nrm(ks[23], (DEPTH, LRU_W), 0.01)
    u = jax.random.uniform(ks[24], (DEPTH, LRU_W), jnp.float32, minval=0.9, maxval=0.999)
    s = u ** (1.0 / LRU_C)
    lru_lambda = jnp.log(s) - jnp.log1p(-s)
    w_branch_a = nrm(ks[25], (DEPTH, ATTN_OUT_W, D_MODEL), BETA * ATTN_OUT_W ** -0.5)
    w_branch_b = nrm(ks[26], (DEPTH, LRU_W, D_MODEL), BETA * LRU_W ** -0.5)
    w_out = nrm(ks[27], (DEPTH, D_MODEL, D_MODEL), BETA * D_MODEL ** -0.5)
    return {"x_prompt": x_prompt, "x_sample": x_sample,
            "cache_kv_w128": caches[0], "cache_kv_w512": caches[1], "cache_kv_w2048": caches[2],
            "state_rglru_h": state_rglru_h, "state_rglru_conv": state_rglru_conv,
            "ln_g": ln_g, "ln_b": ln_b, "ffn1_w_in": ffn1_w_in, "ffn1_w_out": ffn1_w_out,
            "ffn2_w_in": ffn2_w_in, "ffn2_w_out": ffn2_w_out, "w_in": w_in, "conv_w": conv_w, "conv_b": conv_b,
            "lru_wa": lru_wa, "lru_ba": lru_ba, "lru_wx": lru_wx, "lru_bx": lru_bx, "lru_lambda": lru_lambda,
            "w_branch_a": w_branch_a, "w_branch_b": w_branch_b, "w_out": w_out}


def reference(x_prompt, x_sample, cache_kv_w128, cache_kv_w512, cache_kv_w2048, state_rglru_h, state_rglru_conv,
              ln_g, ln_b, ffn1_w_in, ffn1_w_out, ffn2_w_in, ffn2_w_out, w_in, conv_w, conv_b,
              lru_wa, lru_ba, lru_wx, lru_bx, lru_lambda, w_branch_a, w_branch_b, w_out):
    caches = (cache_kv_w128, cache_kv_w512, cache_kv_w2048)
    n_prompt, len_prompt = x_prompt.shape[0], x_prompt.shape[1]
    yp, ys = x_prompt, x_sample
    p_kv = [[] for _ in range(N_GROUPS)]
    s_kv = [[] for _ in range(N_GROUPS)]
    p_h, p_c, s_h, s_c = [], [], [], []
    for l in range(DEPTH):
        params = (ln_g[l], ln_b[l], ffn1_w_in[l], ffn1_w_out[l], ffn2_w_in[l], ffn2_w_out[l], w_in[l],
                  conv_w[l], conv_b[l], lru_wa[l], lru_ba[l], lru_wx[l], lru_bx[l], lru_lambda[l],
                  w_branch_a[l], w_branch_b[l], w_out[l])
        empty = [jnp.zeros((n_prompt, 0, 2, HEADS_PER_GROUP, HEAD_DIM), yp.dtype) for _ in range(N_GROUPS)]
        yp, kvn, h, cb = layer(yp, 0, empty, jnp.zeros((n_prompt, LRU_W), yp.dtype),
                               jnp.zeros((n_prompt, CONV_W - 1, LRU_W), yp.dtype), *params)
        for g, (w, _) in enumerate(ATTN_GROUPS):
            p_kv[g].append(kvn[g][:, len_prompt - min(w, len_prompt):])
        p_h.append(h)
        p_c.append(cb)
        ys, kvn, h, cb = layer(ys, PAST_LEN, [c[l] for c in caches], state_rglru_h[l], state_rglru_conv[l], *params)
        for g in range(N_GROUPS):
            s_kv[g].append(kvn[g])
        s_h.append(h)
        s_c.append(cb)
    prompt_kv_w128 = jnp.stack(p_kv[0])
    prompt_kv_w512 = jnp.stack(p_kv[1])
    prompt_kv_w2048 = jnp.stack(p_kv[2])
    prompt_rglru_h = jnp.stack(p_h)
    prompt_rglru_conv = jnp.stack(p_c)
    sample_kv_w128 = jnp.stack(s_kv[0])
    sample_kv_w512 = jnp.stack(s_kv[1])
    sample_kv_w2048 = jnp.stack(s_kv[2])
    sample_rglru_h = jnp.stack(s_h)
    sample_rglru_conv = jnp.stack(s_c)
    return (yp, ys, prompt_kv_w128, prompt_kv_w512, prompt_kv_w2048, prompt_rglru_h, prompt_rglru_conv,
            sample_kv_w128, sample_kv_w512, sample_kv_w2048, sample_rglru_h, sample_rglru_conv)
```

```python
import functools

import jax
import jax.numpy as jnp
from jax import lax
from jax.experimental import pallas as pl
from jax.experimental.pallas import tpu as pltpu

F32 = jnp.float32
BF16 = jnp.bfloat16

HEAD_DIM = 128
HEADS_PER_GROUP = 4
ATTN_GROUPS = ((128, 1), (512, 4), (2048, 16))
N_GROUPS = len(ATTN_GROUPS)
N_HEADS = N_GROUPS * HEADS_PER_GROUP
ATTN_W = N_HEADS * HEAD_DIM
ATTN_OUT_W = HEADS_PER_GROUP * HEAD_DIM
KEYS_PER_BLOCK = 128
LRU_BLOCKS = 8
CONV_W = 4
LRU_C = 8.0
ROPE_THETA = 10000.0
LN_EPS = 1e-5
NEG = -1e30
SUBLANES = 8
VMEM_LIMIT = 56 * 1024 * 1024

PROJ_TN = 1536
COL_XL, COL_GA, COL_GB = 0, 1, 2
COL_Q, COL_K, COL_V = 2, 3, 4
HEAD_COL_Q = 3 * 1024 // HEAD_DIM
HEAD_COL_K = HEAD_COL_Q + N_HEADS
HEAD_COL_V = HEAD_COL_K + N_HEADS


def _row_tile(m, cap):
    t = cap
    while m % t:
        t //= 2
    return t


def _params(sem):
    return pltpu.CompilerParams(dimension_semantics=sem, vmem_limit_bytes=VMEM_LIMIT)


def _layer_norm(y, g, b):
    mu = jnp.mean(y, axis=-1, keepdims=True)
    d = y - mu
    var = jnp.mean(d * d, axis=-1, keepdims=True)
    return d * lax.rsqrt(var + LN_EPS) * g + b


def _nt_dot(a, b):
    return lax.dot_general(a, b, (((1,), (1,)), ((), ())), preferred_element_type=F32)


def _ffn_ln_kernel(x_ref, wg_ref, wu_ref, wo_ref, g_ref, b_ref, o_ref, xb_ref, acc_ref, *, alpha):
    j = pl.program_id(1)

    @pl.when(j == 0)
    def _():
        xb_ref[...] = x_ref[...].astype(BF16)

    xb = xb_ref[...]
    hg = jnp.dot(xb, wg_ref[...], preferred_element_type=F32)
    hu = jnp.dot(xb, wu_ref[...], preferred_element_type=F32)
    act = (hg * jax.nn.sigmoid(hg) * hu).astype(BF16)
    part = jnp.dot(act, wo_ref[...], preferred_element_type=F32)

    @pl.when(j == 0)
    def _():
        acc_ref[...] = part

    @pl.when(j > 0)
    def _():
        acc_ref[...] += part

    @pl.when(j == pl.num_programs(1) - 1)
    def _():
        y = alpha * x_ref[...] + 0.5 * acc_ref[...]
        o_ref[...] = _layer_norm(y, g_ref[...], b_ref[...])


def _ffn_ln(x, w_in, w_out, g, b, alpha):
    m, d = x.shape
    f = w_out.shape[0]
    tm = _row_tile(m, 512)
    tf = f // 2 if (f // 2) % 128 == 0 else f
    nf = f // tf
    return pl.pallas_call(
        functools.partial(_ffn_ln_kernel, alpha=alpha),
        grid=(m // tm, nf),
        in_specs=[
            pl.BlockSpec((tm, d), lambda i, j: (i, 0)),
            pl.BlockSpec((d, tf), lambda i, j: (0, j)),
            pl.BlockSpec((d, tf), lambda i, j: (0, j + nf)),
            pl.BlockSpec((tf, d), lambda i, j: (j, 0)),
            pl.BlockSpec((1, d), lambda i, j: (0, 0)),
            pl.BlockSpec((1, d), lambda i, j: (0, 0)),
        ],
        out_specs=pl.BlockSpec((tm, d), lambda i, j: (i, 0)),
        out_shape=jax.ShapeDtypeStruct((m, d), F32),
        scratch_shapes=[pltpu.VMEM((tm, d), BF16), pltpu.VMEM((tm, d), F32)],
        compiler_params=_params(("parallel", "arbitrary")),
        name="ffn_ln",
    )(x, w_in, w_in, w_out, g, b)


def _inproj_kernel(x_ref, w_ref, cos_ref, sin_ref, o_ref, xb_ref, *, q_scale):
    j = pl.program_id(1)

    @pl.when(j == 0)
    def _():
        xb_ref[...] = x_ref[...].astype(BF16)

    acc = jnp.dot(xb_ref[...], w_ref[...], preferred_element_type=F32)

    @pl.when(jnp.logical_or(j == COL_Q, j == COL_K))
    def _():
        cos = cos_ref[...]
        sin = sin_ref[...]
        scale = jnp.where(j == COL_Q, q_scale, 1.0).astype(F32)
        for c in range(PROJ_TN // HEAD_DIM):
            h = acc[:, c * HEAD_DIM:(c + 1) * HEAD_DIM]
            r = pltpu.roll(h, HEAD_DIM // 2, axis=1)
            o_ref[:, c * HEAD_DIM:(c + 1) * HEAD_DIM] = (h * cos + r * sin) * scale

    @pl.when(jnp.logical_and(j != COL_Q, j != COL_K))
    def _():
        o_ref[...] = acc


def _inproj(x, w, cos, sin):
    m, d = x.shape
    n = w.shape[1]
    tm = _row_tile(m, 512)
    return pl.pallas_call(
        functools.partial(_inproj_kernel, q_scale=HEAD_DIM ** -0.5),
        grid=(m // tm, n // PROJ_TN),
        in_specs=[
            pl.BlockSpec((tm, d), lambda i, j: (i, 0)),
            pl.BlockSpec((d, PROJ_TN), lambda i, j: (0, j)),
            pl.BlockSpec((tm, HEAD_DIM), lambda i, j: (i, 0)),
            pl.BlockSpec((tm, HEAD_DIM), lambda i, j: (i, 0)),
        ],
        out_specs=pl.BlockSpec((tm, PROJ_TN), lambda i, j: (i, j)),
        out_shape=jax.ShapeDtypeStruct((m, n), F32),
        scratch_shapes=[pltpu.VMEM((tm, d), BF16)],
        compiler_params=_params(("parallel", "arbitrary")),
        name="inproj_rope",
    )(x, w, cos, sin)


def _attn_unit(q, kp, kc, vp, vc, mask_prev, mask_cur):
    q = q.astype(BF16)
    s_p = jnp.where(mask_prev, _nt_dot(q, kp.astype(BF16)), NEG)
    s_c = jnp.where(mask_cur, _nt_dot(q, kc.astype(BF16)), NEG)
    m = jnp.maximum(jnp.max(s_p, axis=-1, keepdims=True), jnp.max(s_c, axis=-1, keepdims=True))
    p_p = jnp.exp(s_p - m)
    p_c = jnp.exp(s_c - m)
    l = jnp.sum(p_p, axis=-1, keepdims=True) + jnp.sum(p_c, axis=-1, keepdims=True)
    o = (jnp.dot(p_p.astype(BF16), vp.astype(BF16), preferred_element_type=F32)
         + jnp.dot(p_c.astype(BF16), vc.astype(BF16), preferred_element_type=F32))
    return o / l, m + jnp.log(l)


def _attn_prompt_kernel(q0, q1, q2, k0, k1, k2, v0, v1, v2, kp0, kp1, kp2, vp0, vp1, vp2,
                        o_ref, og_ref, lse_ref):
    has_prev = pl.program_id(1) > 0
    nq = KEYS_PER_BLOCK
    row = lax.broadcasted_iota(jnp.int32, (nq, nq), 0)
    col = lax.broadcasted_iota(jnp.int32, (nq, nq), 1)
    mask_cur = col <= row
    mask_in = col >= row
    mask_edge = col >= row + jnp.where(has_prev, 0, nq)

    def run(g, dil, q_ref, k_ref, v_ref, kprev_ref, vprev_ref, q_start, prev_start, from_prev_chunk):
        def rows(ref, start):
            if dil == 1:
                return ref[pl.ds(start, nq), :]
            return ref[pl.ds(start, nq, stride=dil), :]

        if from_prev_chunk:
            kp, vp, mp = rows(kprev_ref, prev_start), rows(vprev_ref, prev_start), mask_edge
        else:
            kp, vp, mp = rows(k_ref, prev_start), rows(v_ref, prev_start), mask_in
        o, lse = _attn_unit(rows(q_ref, q_start), kp, rows(k_ref, q_start), vp, rows(v_ref, q_start),
                            mp, mask_cur)
        lse = jnp.broadcast_to(lse, (nq, HEAD_DIM))
        if dil == 1:
            og_ref[g, pl.ds(q_start, nq), :] = o
            lse_ref[g, pl.ds(q_start, nq), :] = lse
        else:
            og_ref[g, pl.ds(q_start, nq, stride=dil), :] = o
            lse_ref[g, pl.ds(q_start, nq, stride=dil), :] = lse

    chunk = q0.shape[0]
    refs = ((q0, k0, v0, kp0, vp0), (q1, k1, v1, kp1, vp1), (q2, k2, v2, kp2, vp2))
    for g, (window, dil) in enumerate(ATTN_GROUPS):
        q_ref, k_ref, v_ref, kprev_ref, vprev_ref = refs[g]
        span = nq * dil
        def first(r, carry, g=g, dil=dil, a=(q_ref, k_ref, v_ref, kprev_ref, vprev_ref)):
            run(g, dil, *a, r, r, True)
            return carry
        lax.fori_loop(0, dil, first, 0)
        n_rest = (chunk // span - 1) * dil
        if n_rest:
            def rest(idx, carry, g=g, dil=dil, span=span, a=(q_ref, k_ref, v_ref, kprev_ref, vprev_ref)):
                sc = idx // dil + 1
                r = idx % dil
                q_start = sc * span + r
                if dil == 1:
                    q_start = pl.multiple_of(q_start, nq)
                run(g, dil, *a, q_start, q_start - span, False)
                return carry
            lax.fori_loop(0, n_rest, rest, 0)

    l0, l1, l2 = lse_ref[0], lse_ref[1], lse_ref[2]
    mx = jnp.maximum(jnp.maximum(l0, l1), l2)
    w0, w1, w2 = jnp.exp(l0 - mx), jnp.exp(l1 - mx), jnp.exp(l2 - mx)
    o_ref[...] = (w0 * og_ref[0] + w1 * og_ref[1] + w2 * og_ref[2]) / (w0 + w1 + w2)


def _attn_prompt(proj, n_seq, seq_len):
    chunk = KEYS_PER_BLOCK * max(d for _, d in ATTN_GROUPS)
    nc = seq_len // chunk
    in_specs = []
    for head_col in (HEAD_COL_Q, HEAD_COL_K, HEAD_COL_V):
        for g in range(N_GROUPS):
            in_specs.append(pl.BlockSpec(
                (chunk, HEAD_DIM),
                lambda b, c, h, hc=head_col, g=g: (b * nc + c, hc + g * HEADS_PER_GROUP + h)))
    for head_col in (HEAD_COL_K, HEAD_COL_V):
        for g, (window, dil) in enumerate(ATTN_GROUPS):
            span = KEYS_PER_BLOCK * dil
            per_chunk = chunk // span
            in_specs.append(pl.BlockSpec(
                (span, HEAD_DIM),
                lambda b, c, h, hc=head_col, g=g, pc=per_chunk:
                (b * nc * pc + jnp.maximum(c * pc - 1, 0), hc + g * HEADS_PER_GROUP + h)))
    return pl.pallas_call(
        _attn_prompt_kernel,
        grid=(n_seq, nc, HEADS_PER_GROUP),
        in_specs=in_specs,
        out_specs=pl.BlockSpec((chunk, HEAD_DIM), lambda b, c, h: (b * nc + c, h)),
        out_shape=jax.ShapeDtypeStruct((n_seq * seq_len, ATTN_OUT_W), F32),
        scratch_shapes=[pltpu.VMEM((N_GROUPS, chunk, HEAD_DIM), F32),
                        pltpu.VMEM((N_GROUPS, chunk, HEAD_DIM), F32)],
        compiler_params=_params(("parallel", "parallel", "parallel")),
        name="attn_prompt",
    )(*([proj] * 15))


def _attn_sample_kernel(q_ref, kn_ref, vn_ref, c0_ref, c1_ref, c2_ref, o_ref, *, cache_lens):
    nt = q_ref.shape[0]
    caches = (c0_ref, c1_ref, c2_ref)
    kv_rows = 2 * HEADS_PER_GROUP
    zpad = jnp.zeros((KEYS_PER_BLOCK - nt, HEAD_DIM), BF16)
    i_new = lax.broadcasted_iota(jnp.int32, (nt, KEYS_PER_BLOCK), 0)
    r_new = lax.broadcasted_iota(jnp.int32, (nt, KEYS_PER_BLOCK), 1)
    d_new = i_new - r_new
    for h in range(HEADS_PER_GROUP):
        ms, ls, accs = [], [], []
        for g, (window, dil) in enumerate(ATTN_GROUPS):
            lc = cache_lens[g]
            c_ref = caches[g]
            col = (g * HEADS_PER_GROUP + h) * HEAD_DIM
            q = q_ref[:, col:col + HEAD_DIM].astype(BF16)
            kc = c_ref[pl.ds(h, lc, stride=kv_rows), :].astype(BF16)
            vc = c_ref[pl.ds(HEADS_PER_GROUP + h, lc, stride=kv_rows), :].astype(BF16)
            kn = jnp.concatenate([kn_ref[:, col:col + HEAD_DIM].astype(BF16), zpad], axis=0)
            vn = jnp.concatenate([vn_ref[:, col:col + HEAD_DIM].astype(BF16), zpad], axis=0)
            i_c = lax.broadcasted_iota(jnp.int32, (nt, lc), 0)
            r_c = lax.broadcasted_iota(jnp.int32, (nt, lc), 1)
            d_c = lc + i_c - r_c
            ok_c = jnp.logical_and(jnp.bitwise_and(d_c, dil - 1) == 0, d_c <= window)
            ok_n = jnp.logical_and(jnp.logical_and(d_new >= 0, jnp.bitwise_and(d_new, dil - 1) == 0),
                                   d_new <= window)
            s_c = jnp.where(ok_c, _nt_dot(q, kc), NEG)
            s_n = jnp.where(ok_n, _nt_dot(q, kn), NEG)
            m = jnp.maximum(jnp.max(s_c, axis=-1, keepdims=True), jnp.max(s_n, axis=-1, keepdims=True))
            p_c = jnp.exp(s_c - m)
            p_n = jnp.exp(s_n - m)
            ls.append(jnp.sum(p_c, axis=-1, keepdims=True) + jnp.sum(p_n, axis=-1, keepdims=True))
            accs.append(jnp.dot(p_c.astype(BF16), vc, preferred_element_type=F32)
                        + jnp.dot(p_n.astype(BF16), vn, preferred_element_type=F32))
            ms.append(m)
        mx = jnp.maximum(jnp.maximum(ms[0], ms[1]), ms[2])
        es = [jnp.exp(m - mx) for m in ms]
        num = es[0] * accs[0] + es[1] * accs[1] + es[2] * accs[2]
        den = es[0] * ls[0] + es[1] * ls[1] + es[2] * ls[2]
        o_ref[:, h * HEAD_DIM:(h + 1) * HEAD_DIM] = num / den


def _attn_sample(proj, caches, cache_lens, layer, row0, n_seq, n_tok):
    kv_rows = 2 * HEADS_PER_GROUP
    blk0 = row0 // n_tok
    in_specs = [pl.BlockSpec((n_tok, ATTN_W), lambda b, cb=cb: (blk0 + b, cb)) for cb in (COL_Q, COL_K, COL_V)]
    for g in range(N_GROUPS):
        in_specs.append(pl.BlockSpec((cache_lens[g] * kv_rows, HEAD_DIM),
                                     lambda b: (layer * n_seq + b, 0)))
    return pl.pallas_call(
        functools.partial(_attn_sample_kernel, cache_lens=cache_lens),
        grid=(n_seq,),
        in_specs=in_specs,
        out_specs=pl.BlockSpec((n_tok, ATTN_OUT_W), lambda b: (b, 0)),
        out_shape=jax.ShapeDtypeStruct((n_seq * n_tok, ATTN_OUT_W), F32),
        compiler_params=_params(("parallel",)),
        name="attn_sample",
    )(proj, proj, proj, *caches)


def _lru_coeffs(xc, wa_ref, wx_ref, ba, bx, lam):
    nb = wa_ref.shape[0]
    bs = wa_ref.shape[1]
    xcb = xc.astype(BF16)
    ra, rx = [], []
    for n in range(nb):
        xs = xcb[:, n * bs:(n + 1) * bs]
        ra.append(jnp.dot(xs, wa_ref[n], preferred_element_type=F32))
        rx.append(jnp.dot(xs, wx_ref[n], preferred_element_type=F32))
    r = jax.nn.sigmoid(jnp.concatenate(ra, axis=1) + ba)
    i = jax.nn.sigmoid(jnp.concatenate(rx, axis=1) + bx)
    softplus_neg_lam = jnp.maximum(-lam, 0.0) + jnp.log1p(jnp.exp(-jnp.abs(lam)))
    log_a = -LRU_C * r * softplus_neg_lam
    a = jnp.exp(log_a)
    b = jnp.sqrt(jnp.tanh(-log_a) * (1.0 + a * a)) * i * xc
    return a, b


def _group_scan(a, b):
    t, c = a.shape
    a3 = a.reshape(t // SUBLANES, SUBLANES, c)
    b3 = b.reshape(t // SUBLANES, SUBLANES, c)
    sub = lax.broadcasted_iota(jnp.int32, a3.shape, 1)
    s = 1
    while s < SUBLANES:
        keep = sub >= s
        a_sh = pltpu.roll(a3, s, axis=1)
        b_sh = pltpu.roll(b3, s, axis=1)
        b3 = jnp.where(keep, a3 * b_sh + b3, b3)
        a3 = jnp.where(keep, a3 * a_sh, a3)
        s *= 2
    return a3.reshape(t, c), b3.reshape(t, c)


def _lru_prompt_kernel(x_ref, cw_ref, cb_ref, wa_ref, wx_ref, ba_ref, bx_ref, lam_ref, o_ref,
                       tail_ref, carry_ref, a_ref, b_ref):
    t, c = x_ref.shape

    @pl.when(pl.program_id(2) == 0)
    def _():
        tail_ref[...] = jnp.zeros_like(tail_ref)
        carry_ref[...] = jnp.zeros_like(carry_ref)

    x = x_ref[...]
    tail = tail_ref[...]
    sub = lax.broadcasted_iota(jnp.int32, (SUBLANES, c), 0)
    xc = cb_ref[...] + cw_ref[CONV_W - 1:CONV_W, :] * x
    for k in range(1, CONV_W):
        xr = pltpu.roll(x, k, axis=0)
        first = jnp.where(sub >= k, xr[:SUBLANES], pltpu.roll(tail, k, axis=0))
        xs = jnp.concatenate([first, xr[SUBLANES:]], axis=0)
        xc = xc + cw_ref[CONV_W - 1 - k:CONV_W - k, :] * xs
    tail_ref[...] = x[t - SUBLANES:]

    a, b = _lru_coeffs(xc, wa_ref, wx_ref, ba_ref[...], bx_ref[...], lam_ref[...])
    a, b = _group_scan(a, b)
    a_ref[...] = a
    b_ref[...] = b

    def body(g, h):
        st = pl.multiple_of(g * SUBLANES, SUBLANES)
        hg = a_ref[pl.ds(st, SUBLANES), :] * h + b_ref[pl.ds(st, SUBLANES), :]
        o_ref[pl.ds(st, SUBLANES), :] = hg
        return jnp.broadcast_to(hg[SUBLANES - 1:, :], hg.shape)

    carry_ref[...] = lax.fori_loop(0, t // SUBLANES, body, carry_ref[...], unroll=8)


def _lru_specs(c_half):
    nb_half = LRU_BLOCKS // 2
    return [
        pl.BlockSpec((CONV_W, c_half), lambda *ids: (0, ids[-2])),
        pl.BlockSpec((1, c_half), lambda *ids: (0, ids[-2])),
        pl.BlockSpec((nb_half, c_half // nb_half, c_half // nb_half), lambda *ids: (ids[-2], 0, 0)),
        pl.BlockSpec((nb_half, c_half // nb_half, c_half // nb_half), lambda *ids: (ids[-2], 0, 0)),
        pl.BlockSpec((1, c_half), lambda *ids: (0, ids[-2])),
        pl.BlockSpec((1, c_half), lambda *ids: (0, ids[-2])),
        pl.BlockSpec((1, c_half), lambda *ids: (0, ids[-2])),
    ]


def _lru_prompt(proj, lru_w, n_seq, seq_len, width):
    t = _row_tile(seq_len, 1024)
    nt = seq_len // t
    ch = width // 2
    return pl.pallas_call(
        _lru_prompt_kernel,
        grid=(n_seq, 2, nt),
        in_specs=[pl.BlockSpec((t, ch), lambda b, hf, i: (b * nt + i, COL_XL * 2 + hf))] + _lru_specs(ch),
        out_specs=pl.BlockSpec((t, ch), lambda b, hf, i: (b * nt + i, hf)),
        out_shape=jax.ShapeDtypeStruct((n_seq * seq_len, width), F32),
        scratch_shapes=[pltpu.VMEM((SUBLANES, ch), F32), pltpu.VMEM((SUBLANES, ch), F32),
                        pltpu.VMEM((t, ch), F32), pltpu.VMEM((t, ch), F32)],
        compiler_params=_params(("parallel", "parallel", "arbitrary")),
        name="lru_prompt",
    )(proj, *lru_w)


def _lru_sample_kernel(x_ref, prev_ref, h0_ref, cw_ref, cb_ref, wa_ref, wx_ref, ba_ref, bx_ref, lam_ref, o_ref):
    t, c = x_ref.shape
    x3 = x_ref[...].reshape(t // SUBLANES, SUBLANES, c)
    p3 = prev_ref[...].reshape(t // SUBLANES, SUBLANES, c)
    sub = lax.broadcasted_iota(jnp.int32, x3.shape, 1)
    xc = cb_ref[...] + cw_ref[CONV_W - 1:CONV_W, :] * x3
    for k in range(1, CONV_W):
        xs = jnp.where(sub >= k, pltpu.roll(x3, k, axis=1), pltpu.roll(p3, k, axis=1))
        xc = xc + cw_ref[CONV_W - 1 - k:CONV_W - k, :] * xs
    xc = xc.reshape(t, c)
    a, b = _lru_coeffs(xc, wa_ref, wx_ref, ba_ref[...], bx_ref[...], lam_ref[...])
    a, b = _group_scan(a, b)
    o_ref[...] = a * h0_ref[...] + b


def _lru_sample(proj, prev, h0, lru_w, row0, n_rows, width):
    t = _row_tile(n_rows, 512)
    ch = width // 2
    blk0 = row0 // t
    return pl.pallas_call(
        _lru_sample_kernel,
        grid=(n_rows // t, 2, 1),
        in_specs=[pl.BlockSpec((t, ch), lambda i, hf, z: (blk0 + i, COL_XL * 2 + hf)),
                  pl.BlockSpec((t, ch), lambda i, hf, z: (i, hf)),
                  pl.BlockSpec((t, ch), lambda i, hf, z: (i, hf))] + _lru_specs(ch),
        out_specs=pl.BlockSpec((t, ch), lambda i, hf, z: (i, hf)),
        out_shape=jax.ShapeDtypeStruct((n_rows, width), F32),
        compiler_params=_params(("parallel", "parallel", "arbitrary")),
        name="lru_sample",
    )(proj, prev, h0, *lru_w)


def _merge_kernel(x_ref, oa_ref, ob_ref, ga_ref, gb_ref, wa_ref, wb_ref, wo_ref, g_ref, b_ref, o_ref, *, alpha):
    ya = jnp.dot(oa_ref[...].astype(BF16), wa_ref[...], preferred_element_type=F32)
    yb = jnp.dot(ob_ref[...].astype(BF16), wb_ref[...], preferred_element_type=F32)
    z = jax.nn.sigmoid(ga_ref[...]) * ya + jax.nn.sigmoid(gb_ref[...]) * yb
    m = jnp.dot(z.astype(BF16), wo_ref[...], preferred_element_type=F32)
    o_ref[...] = _layer_norm(alpha * x_ref[...] + m, g_ref[...], b_ref[...])


def _merge(x, o_a, o_b, proj, wa, wb, wo, g, b, alpha):
    m, d = x.shape
    tm = _row_tile(m, 512)
    const = lambda i: (0, 0)
    return pl.pallas_call(
        functools.partial(_merge_kernel, alpha=alpha),
        grid=(m // tm,),
        in_specs=[
            pl.BlockSpec((tm, d), lambda i: (i, 0)),
            pl.BlockSpec((tm, o_a.shape[1]), lambda i: (i, 0)),
            pl.BlockSpec((tm, d), lambda i: (i, 0)),
            pl.BlockSpec((tm, d), lambda i: (i, COL_GA)),
            pl.BlockSpec((tm, d), lambda i: (i, COL_GB)),
            pl.BlockSpec(wa.shape, const),
            pl.BlockSpec(wb.shape, const),
            pl.BlockSpec(wo.shape, const),
            pl.BlockSpec((1, d), const),
            pl.BlockSpec((1, d), const),
        ],
        out_specs=pl.BlockSpec((tm, d), lambda i: (i, 0)),
        out_shape=jax.ShapeDtypeStruct((m, d), F32),
        compiler_params=_params(("parallel",)),
        name="merge_ln",
    )(x, o_a, o_b, proj, proj, wa, wb, wo, g, b)


def kernel(x_prompt, x_sample, cache_kv_w128, cache_kv_w512, cache_kv_w2048, state_rglru_h, state_rglru_conv,
           ln_g, ln_b, ffn1_w_in, ffn1_w_out, ffn2_w_in, ffn2_w_out, w_in, conv_w, conv_b,
           lru_wa, lru_ba, lru_wx, lru_bx, lru_lambda, w_branch_a, w_branch_b, w_out):
    n_p, len_p, d = x_prompt.shape
    n_s, len_s, _ = x_sample.shape
    depth = w_in.shape[0]
    alpha = (2 * depth) ** 0.25
    m_p, m_s = n_p * len_p, n_s * len_s
    caches_in = (cache_kv_w128, cache_kv_w512, cache_kv_w2048)
    past_len = max(c.shape[2] for c in caches_in)
    assert len_s == SUBLANES, "the sample kernels treat one 8-row group as one sequence"

    half = HEAD_DIM // 2
    inv = ROPE_THETA ** (-jnp.arange(half, dtype=F32) / half)
    pos = jnp.concatenate([jnp.tile(jnp.arange(len_p, dtype=F32), n_p),
                           jnp.tile(past_len + jnp.arange(len_s, dtype=F32), n_s)])
    ang = pos[:, None] * inv[None, :]
    cos = jnp.concatenate([jnp.cos(ang), jnp.cos(ang)], axis=1)
    sin = jnp.concatenate([-jnp.sin(ang), jnp.sin(ang)], axis=1)

    attn_cols = 3 * ATTN_W
    w_perm = jnp.concatenate([w_in[:, :, attn_cols:], w_in[:, :, :attn_cols]], axis=-1).astype(BF16)
    cache_rows = tuple(c.reshape(-1, HEAD_DIM) for c in caches_in)
    cache_lens = tuple(c.shape[2] for c in caches_in)
    conv_prev = jnp.pad(state_rglru_conv, ((0, 0), (0, 0), (SUBLANES - (CONV_W - 1), 0), (0, 0)))
    conv_prev = conv_prev.reshape(depth, m_s, d)
    h0_rows = jnp.repeat(state_rglru_h, len_s, axis=1)

    x = jnp.concatenate([x_prompt.reshape(m_p, d), x_sample.reshape(m_s, d)], axis=0)
    outs = {k: [] for k in ("p_kv0", "p_kv1", "p_kv2", "p_h", "p_c", "s_kv0", "s_kv1", "s_kv2", "s_h", "s_c")}
    for l in range(depth):
        g = ln_g[l].reshape(3, 1, d)
        b = ln_b[l].reshape(3, 1, d)
        x = _ffn_ln(x, ffn1_w_in[l].astype(BF16), ffn1_w_out[l].astype(BF16), g[0], b[0], alpha)
        proj = _inproj(x, w_perm[l], cos, sin)
        lru_w = (conv_w[l], conv_b[l].reshape(1, d), lru_wa[l].astype(BF16), lru_wx[l].astype(BF16),
                 lru_ba[l].reshape(1, d), lru_bx[l].reshape(1, d), lru_lambda[l].reshape(1, d))
        oa_p = _attn_prompt(proj, n_p, len_p)
        oa_s = _attn_sample(proj, cache_rows, cache_lens, l, m_p, n_s, len_s)
        ob_p = _lru_prompt(proj, lru_w, n_p, len_p, d)
        ob_s = _lru_sample(proj, conv_prev[l], h0_rows[l], lru_w, m_p, m_s, d)
        o_a = jnp.concatenate([oa_p, oa_s], axis=0)
        o_b = jnp.concatenate([ob_p, ob_s], axis=0)
        x = _merge(x, o_a, o_b, proj, w_branch_a[l].astype(BF16), w_branch_b[l].astype(BF16),
                   w_out[l].astype(BF16), g[1], b[1], alpha)
        x = _ffn_ln(x, ffn2_w_in[l].astype(BF16), ffn2_w_out[l].astype(BF16), g[2], b[2], alpha)

        k_all = proj[:, COL_K * ATTN_W:(COL_K + 1) * ATTN_W]
        v_all = proj[:, COL_V * ATTN_W:(COL_V + 1) * ATTN_W]
        xl = proj[:, :d]
        for gi, (window, _) in enumerate(ATTN_GROUPS):
            cols = slice(gi * ATTN_OUT_W, (gi + 1) * ATTN_OUT_W)
            keep = min(window, len_p)
            kp = k_all[:m_p, cols].reshape(n_p, len_p, HEADS_PER_GROUP, HEAD_DIM)[:, len_p - keep:]
            vp = v_all[:m_p, cols].reshape(n_p, len_p, HEADS_PER_GROUP, HEAD_DIM)[:, len_p - keep:]
            outs[f"p_kv{gi}"].append(jnp.stack([kp, vp], axis=2))
            ks = k_all[m_p:, cols].reshape(n_s, len_s, HEADS_PER_GROUP, HEAD_DIM)
            vs = v_all[m_p:, cols].reshape(n_s, len_s, HEADS_PER_GROUP, HEAD_DIM)
            outs[f"s_kv{gi}"].append(jnp.stack([ks, vs], axis=2))
        outs["p_h"].append(ob_p.reshape(n_p, len_p, d)[:, -1])
        outs["p_c"].append(xl[:m_p].reshape(n_p, len_p, d)[:, len_p - (CONV_W - 1):])
        outs["s_h"].append(ob_s.reshape(n_s, len_s, d)[:, -1])
        outs["s_c"].append(xl[m_p:].reshape(n_s, len_s, d)[:, len_s - (CONV_W - 1):])

    st = {k: jnp.stack(v) for k, v in outs.items()}
    return (x[:m_p].reshape(n_p, len_p, d), x[m_p:].reshape(n_s, len_s, d),
            st["p_kv0"], st["p_kv1"], st["p_kv2"], st["p_h"], st["p_c"],
            st["s_kv0"], st["s_kv1"], st["s_kv2"], st["s_h"], st["s_c"])
```

```python
import functools

import jax
import jax.numpy as jnp
from jax import lax
from jax.experimental import pallas as pl
from jax.experimental.pallas import tpu as pltpu

F32 = jnp.float32
BF16 = jnp.bfloat16

HEAD_DIM = 128
HEADS_PER_GROUP = 4
ATTN_GROUPS = ((128, 1), (512, 4), (2048, 16))
N_GROUPS = len(ATTN_GROUPS)
N_HEADS = N_GROUPS * HEADS_PER_GROUP
ATTN_W = N_HEADS * HEAD_DIM
ATTN_OUT_W = HEADS_PER_GROUP * HEAD_DIM
KEYS_PER_BLOCK = 128
LRU_BLOCKS = 8
CONV_W = 4
LRU_C = 8.0
ROPE_THETA = 10000.0
LN_EPS = 1e-5
NEG = -1e30
SUBLANES = 8
VMEM_LIMIT = 56 * 1024 * 1024

PROJ_TN = ATTN_W
HEAD_COL_Q, HEAD_COL_K, HEAD_COL_V = 0, N_HEADS, 2 * N_HEADS
COL_XL, COL_GA, COL_GB = 0, 1, 2


def _row_tile(m, cap):
    t = cap
    while m % t:
        t //= 2
    return t


def _params(sem):
    return pltpu.CompilerParams(dimension_semantics=sem, vmem_limit_bytes=VMEM_LIMIT)


def _layer_norm(y, g, b):
    mu = jnp.mean(y, axis=-1, keepdims=True)
    d = y - mu
    var = jnp.mean(d * d, axis=-1, keepdims=True)
    return d * lax.rsqrt(var + LN_EPS) * g + b


def _nt_dot(a, b):
    return lax.dot_general(a, b, (((1,), (1,)), ((), ())), preferred_element_type=F32)


def _ffn_ln_kernel(x_ref, wg_ref, wu_ref, wo_ref, g_ref, b_ref, o_ref, xb_ref, acc_ref, *, alpha):
    j = pl.program_id(1)

    @pl.when(j == 0)
    def _():
        xb_ref[...] = x_ref[...].astype(BF16)

    xb = xb_ref[...]
    hg = jnp.dot(xb, wg_ref[...], preferred_element_type=F32)
    hu = jnp.dot(xb, wu_ref[...], preferred_element_type=F32)
    act = (hg * jax.nn.sigmoid(hg) * hu).astype(BF16)
    part = jnp.dot(act, wo_ref[...], preferred_element_type=F32)

    @pl.when(j == 0)
    def _():
        acc_ref[...] = part

    @pl.when(j > 0)
    def _():
        acc_ref[...] += part

    @pl.when(j == pl.num_programs(1) - 1)
    def _():
        y = alpha * x_ref[...] + 0.5 * acc_ref[...]
        o_ref[...] = _layer_norm(y, g_ref[...], b_ref[...])


def _ffn_ln(x, w_in, w_out, g, b, alpha):
    m, d = x.shape
    f = w_out.shape[0]
    tm = _row_tile(m, 512)
    tf = f // 2 if (f // 2) % 128 == 0 else f
    nf = f // tf
    return pl.pallas_call(
        functools.partial(_ffn_ln_kernel, alpha=alpha),
        grid=(m // tm, nf),
        in_specs=[
            pl.BlockSpec((tm, d), lambda i, j: (i, 0)),
            pl.BlockSpec((d, tf), lambda i, j: (0, j)),
            pl.BlockSpec((d, tf), lambda i, j: (0, j + nf)),
            pl.BlockSpec((tf, d), lambda i, j: (j, 0)),
            pl.BlockSpec((1, d), lambda i, j: (0, 0)),
            pl.BlockSpec((1, d), lambda i, j: (0, 0)),
        ],
        out_specs=pl.BlockSpec((tm, d), lambda i, j: (i, 0)),
        out_shape=jax.ShapeDtypeStruct((m, d), F32),
        scratch_shapes=[pltpu.VMEM((tm, d), BF16), pltpu.VMEM((tm, d), F32)],
        compiler_params=_params(("parallel", "arbitrary")),
        name="ffn_ln",
    )(x, w_in, w_in, w_out, g, b)


def _proj_gates_kernel(x_ref, w_ref, o_ref, xb_ref):
    @pl.when(pl.program_id(1) == 0)
    def _():
        xb_ref[...] = x_ref[...].astype(BF16)

    o_ref[...] = jnp.dot(xb_ref[...], w_ref[...], preferred_element_type=F32).astype(o_ref.dtype)


def _proj_qkv_kernel(x_ref, w_ref, cos_ref, sin_ref, o_ref, xb_ref, *, q_scale):
    j = pl.program_id(1)

    @pl.when(j == 0)
    def _():
        xb_ref[...] = x_ref[...].astype(BF16)

    acc = jnp.dot(xb_ref[...], w_ref[...], preferred_element_type=F32)

    @pl.when(j < 2)
    def _():
        cos = cos_ref[...]
        sin = sin_ref[...]
        scale = jnp.where(j == 0, q_scale, 1.0).astype(F32)
        for c in range(PROJ_TN // HEAD_DIM):
            h = acc[:, c * HEAD_DIM:(c + 1) * HEAD_DIM]
            r = pltpu.roll(h, HEAD_DIM // 2, axis=1)
            o_ref[:, c * HEAD_DIM:(c + 1) * HEAD_DIM] = (h * cos + r * sin) * scale

    @pl.when(j == 2)
    def _():
        o_ref[...] = acc


def _proj_gates(x, w):
    m, d = x.shape
    n = w.shape[1]
    tm = _row_tile(m, 1024)
    return pl.pallas_call(
        _proj_gates_kernel,
        grid=(m // tm, n // PROJ_TN),
        in_specs=[pl.BlockSpec((tm, d), lambda i, j: (i, 0)),
                  pl.BlockSpec((d, PROJ_TN), lambda i, j: (0, j))],
        out_specs=pl.BlockSpec((tm, PROJ_TN), lambda i, j: (i, j)),
        out_shape=jax.ShapeDtypeStruct((m, n), BF16),
        scratch_shapes=[pltpu.VMEM((tm, d), BF16)],
        compiler_params=_params(("parallel", "arbitrary")),
        name="proj_gates",
    )(x, w)


def _proj_qkv(x, w, cos, sin, prompt_rows, seq_len):
    m, d = x.shape
    n = w.shape[1]
    tm = cos.shape[0] - seq_len
    tiles_per_seq = seq_len // tm
    prompt_tiles = prompt_rows // tm

    def table_map(i, j):
        return (jnp.where(i < prompt_tiles, i % tiles_per_seq, tiles_per_seq), 0)

    return pl.pallas_call(
        functools.partial(_proj_qkv_kernel, q_scale=HEAD_DIM ** -0.5),
        grid=(m // tm, n // PROJ_TN),
        in_specs=[pl.BlockSpec((tm, d), lambda i, j: (i, 0)),
                  pl.BlockSpec((d, PROJ_TN), lambda i, j: (0, j)),
                  pl.BlockSpec((tm, HEAD_DIM), table_map),
                  pl.BlockSpec((tm, HEAD_DIM), table_map)],
        out_specs=pl.BlockSpec((tm, PROJ_TN), lambda i, j: (i, j)),
        out_shape=jax.ShapeDtypeStruct((m, n), F32),
        scratch_shapes=[pltpu.VMEM((tm, d), BF16)],
        compiler_params=_params(("parallel", "arbitrary")),
        name="proj_qkv_rope",
    )(x, w, cos, sin)


def _attn_units(units):
    scores = [_nt_dot(q, k2) + bias for q, k2, _, bias in units]
    stats = []
    for s in scores:
        m = jnp.max(s, axis=-1, keepdims=True)
        p = jnp.exp(s - m)
        stats.append((m, p.astype(BF16), jnp.sum(p, axis=-1, keepdims=True)))
    outs = [jnp.dot(p, v2, preferred_element_type=F32) for (_, p, _), (_, _, v2, _) in zip(stats, units)]
    return [(o / l, m + jnp.log(l)) for o, (m, _, l) in zip(outs, stats)]


def _attn_prompt_kernel(q0, q1, q2, k0, k1, k2, v0, v1, v2, kp0, kp1, kp2, vp0, vp1, vp2,
                        o_ref, og_ref, lse_ref, *, batch):
    has_prev = pl.program_id(1) > 0
    nq = KEYS_PER_BLOCK
    row = lax.broadcasted_iota(jnp.int32, (nq, 2 * nq), 0)
    col = lax.broadcasted_iota(jnp.int32, (nq, 2 * nq), 1)
    bias_in = jnp.where(jnp.logical_and(col >= row, col <= row + nq), 0.0, NEG).astype(F32)
    first_visible = jnp.where(has_prev, row, nq)
    bias_edge = jnp.where(jnp.logical_and(col >= first_visible, col <= row + nq), 0.0, NEG).astype(F32)

    def run(g, dil, refs, starts, from_prev_chunk):
        q_ref, k_ref, v_ref, kprev_ref, vprev_ref = refs

        def rows(ref, start):
            if dil == 1:
                return ref[pl.ds(start, nq), :]
            return ref[pl.ds(start, nq, stride=dil), :]

        units = []
        for q_start, prev_start in starts:
            if from_prev_chunk:
                kp, vp, bias = rows(kprev_ref, prev_start), rows(vprev_ref, prev_start), bias_edge
            else:
                kp, vp, bias = rows(k_ref, prev_start), rows(v_ref, prev_start), bias_in
            units.append((rows(q_ref, q_start).astype(BF16),
                          jnp.concatenate([kp, rows(k_ref, q_start)], axis=0).astype(BF16),
                          jnp.concatenate([vp, rows(v_ref, q_start)], axis=0).astype(BF16), bias))
        for (q_start, _), (o, lse) in zip(starts, _attn_units(units)):
            lse = jnp.broadcast_to(lse, (nq, HEAD_DIM))
            if dil == 1:
                og_ref[g, pl.ds(q_start, nq), :] = o
                lse_ref[g, pl.ds(q_start, nq), :] = lse
            else:
                og_ref[g, pl.ds(q_start, nq, stride=dil), :] = o
                lse_ref[g, pl.ds(q_start, nq, stride=dil), :] = lse

    def batch_of(n):
        return max(u for u in range(1, batch + 1) if n % u == 0)

    chunk = q0.shape[0]
    all_refs = ((q0, k0, v0, kp0, vp0), (q1, k1, v1, kp1, vp1), (q2, k2, v2, kp2, vp2))
    for g, (window, dil) in enumerate(ATTN_GROUPS):
        span = nq * dil
        refs = all_refs[g]

        u1 = batch_of(dil)

        def first(it, carry, g=g, dil=dil, refs=refs, u1=u1):
            run(g, dil, refs, [(it * u1 + u, it * u1 + u) for u in range(u1)], True)
            return carry
        lax.fori_loop(0, dil // u1, first, 0)
        n_rest = (chunk // span - 1) * dil
        if n_rest:
            u2 = batch_of(n_rest)

            def rest(it, carry, g=g, dil=dil, span=span, refs=refs, u2=u2):
                starts = []
                for u in range(u2):
                    idx = it * u2 + u
                    q_start = (idx // dil + 1) * span + idx % dil
                    if dil == 1:
                        q_start = pl.multiple_of(q_start, nq)
                    starts.append((q_start, q_start - span))
                run(g, dil, refs, starts, False)
                return carry
            lax.fori_loop(0, n_rest // u2, rest, 0)

    l0, l1, l2 = lse_ref[0], lse_ref[1], lse_ref[2]
    mx = jnp.maximum(jnp.maximum(l0, l1), l2)
    w0, w1, w2 = jnp.exp(l0 - mx), jnp.exp(l1 - mx), jnp.exp(l2 - mx)
    o_ref[...] = (w0 * og_ref[0] + w1 * og_ref[1] + w2 * og_ref[2]) / (w0 + w1 + w2)


def _attn_prompt(qkv, n_seq, seq_len):
    chunk = KEYS_PER_BLOCK * max(d for _, d in ATTN_GROUPS)
    assert seq_len % chunk == 0
    nc = seq_len // chunk
    in_specs = []
    for head_col in (HEAD_COL_Q, HEAD_COL_K, HEAD_COL_V):
        for g in range(N_GROUPS):
            in_specs.append(pl.BlockSpec(
                (chunk, HEAD_DIM),
                lambda b, c, h, hc=head_col, g=g: (b * nc + c, hc + g * HEADS_PER_GROUP + h)))
    for head_col in (HEAD_COL_K, HEAD_COL_V):
        for g, (window, dil) in enumerate(ATTN_GROUPS):
            span = KEYS_PER_BLOCK * dil
            per_chunk = chunk // span
            in_specs.append(pl.BlockSpec(
                (span, HEAD_DIM),
                lambda b, c, h, hc=head_col, g=g, pc=per_chunk:
                (b * nc * pc + jnp.maximum(c * pc - 1, 0), hc + g * HEADS_PER_GROUP + h)))
    return pl.pallas_call(
        functools.partial(_attn_prompt_kernel, batch=8),
        grid=(n_seq, nc, HEADS_PER_GROUP),
        in_specs=in_specs,
        out_specs=pl.BlockSpec((chunk, HEAD_DIM), lambda b, c, h: (b * nc + c, h)),
        out_shape=jax.ShapeDtypeStruct((qkv.shape[0], ATTN_OUT_W), F32),
        scratch_shapes=[pltpu.VMEM((N_GROUPS, chunk, HEAD_DIM), F32),
                        pltpu.VMEM((N_GROUPS, chunk, HEAD_DIM), F32)],
        compiler_params=_params(("parallel", "parallel", "parallel")),
        name="attn_prompt",
    )(*([qkv] * 15))


def _attn_sample_kernel(q_ref, kn_ref, vn_ref, c0_ref, c1_ref, c2_ref, o_in_ref, o_ref, *, cache_lens):
    del o_in_ref
    nt = q_ref.shape[0]
    nh = HEADS_PER_GROUP
    caches = (c0_ref, c1_ref, c2_ref)
    zpad = jnp.zeros((KEYS_PER_BLOCK - nt, HEAD_DIM), BF16)

    kvs, vns, s_cache, s_new = {}, {}, [], []
    for g in range(N_GROUPS):
        lc = cache_lens[g]
        sc, sn = [], []
        for h in range(nh):
            col = (g * nh + h) * HEAD_DIM
            q = q_ref[:, col:col + HEAD_DIM].astype(BF16)
            if len(caches[g].shape) == 2:
                kv = caches[g][pl.ds(h, 2 * lc, stride=nh), :]
            else:
                kv = caches[g][:, pl.ds(h, 2 * nt, stride=nh), :]
                kv = kv.reshape(kv.shape[0] * kv.shape[1], HEAD_DIM)
            kvs[g, h] = kv.astype(BF16)
            kn = jnp.concatenate([kn_ref[:, col:col + HEAD_DIM].astype(BF16), zpad], axis=0)
            vns[g, h] = jnp.concatenate([vn_ref[:, col:col + HEAD_DIM].astype(BF16), zpad], axis=0)
            sc.append(_nt_dot(q, kvs[g, h]))
            sn.append(_nt_dot(q, kn))
        s_cache.append(jnp.concatenate(sc, axis=0))
        s_new.append(jnp.concatenate(sn, axis=0))

    rows = nh * nt
    i_new = jnp.bitwise_and(lax.broadcasted_iota(jnp.int32, (rows, KEYS_PER_BLOCK), 0), nt - 1)
    d_new = i_new - lax.broadcasted_iota(jnp.int32, (rows, KEYS_PER_BLOCK), 1)
    ms, ls, p_vs, p_ns = [], [], [], []
    for g, (window, dil) in enumerate(ATTN_GROUPS):
        lc = cache_lens[g]
        n_kv = s_cache[g].shape[1]
        i_c = jnp.bitwise_and(lax.broadcasted_iota(jnp.int32, (rows, n_kv), 0), nt - 1)
        c_c = lax.broadcasted_iota(jnp.int32, (rows, n_kv), 1)
        if len(caches[g].shape) == 2:
            pos = jnp.right_shift(c_c, 1)
        else:
            shift = (2 * nt).bit_length() - 1
            pos = dil * jnp.right_shift(c_c, shift) + jnp.bitwise_and(jnp.right_shift(c_c, 1), nt - 1)
        d_c = lc + i_c - pos
        ok_c = jnp.logical_and(jnp.bitwise_and(c_c, 1) == 0,
                               jnp.logical_and(jnp.bitwise_and(d_c, dil - 1) == 0, d_c <= window))
        ok_n = jnp.logical_and(jnp.logical_and(d_new >= 0, jnp.bitwise_and(d_new, dil - 1) == 0),
                               d_new <= window)
        s_c = jnp.where(ok_c, s_cache[g], NEG)
        s_n = jnp.where(ok_n, s_new[g], NEG)
        m = jnp.maximum(jnp.max(s_c, axis=-1, keepdims=True), jnp.max(s_n, axis=-1, keepdims=True))
        p_c = jnp.exp(s_c - m)
        p_n = jnp.exp(s_n - m)
        ms.append(m)
        ls.append(jnp.sum(p_c, axis=-1, keepdims=True) + jnp.sum(p_n, axis=-1, keepdims=True))
        p_vs.append(pltpu.roll(p_c, 1, axis=1))
        p_ns.append(p_n)

    accs = []
    for g in range(N_GROUPS):
        accs.append(jnp.concatenate(
            [jnp.dot(p_vs[g][h * nt:(h + 1) * nt].astype(BF16), kvs[g, h], preferred_element_type=F32)
             + jnp.dot(p_ns[g][h * nt:(h + 1) * nt].astype(BF16), vns[g, h], preferred_element_type=F32)
             for h in range(nh)], axis=0))

    mx = jnp.maximum(jnp.maximum(ms[0], ms[1]), ms[2])
    es = [jnp.exp(m - mx) for m in ms]
    num = es[0] * accs[0] + es[1] * accs[1] + es[2] * accs[2]
    den = es[0] * ls[0] + es[1] * ls[1] + es[2] * ls[2]
    out = num / den
    for h in range(nh):
        o_ref[:, h * HEAD_DIM:(h + 1) * HEAD_DIM] = out[h * nt:(h + 1) * nt]


def _attn_sample(qkv, caches, cache_lens, o_a, layer, row0, n_seq, n_tok):
    kv_rows = 2 * HEADS_PER_GROUP
    blk0 = row0 // n_tok
    in_specs = [pl.BlockSpec((n_tok, ATTN_W), lambda b, cb=cb: (blk0 + b, cb)) for cb in range(3)]
    views = []
    for g, (window, dil) in enumerate(ATTN_GROUPS):
        lc = cache_lens[g]
        if dil > n_tok and lc % dil == 0:
            periods = lc // dil
            views.append(caches[g].reshape(-1, dil * kv_rows, HEAD_DIM))
            in_specs.append(pl.BlockSpec((periods, n_tok * kv_rows, HEAD_DIM),
                                         lambda b: (layer * n_seq + b, 0, 0)))
        else:
            views.append(caches[g].reshape(-1, HEAD_DIM))
            in_specs.append(pl.BlockSpec((lc * kv_rows, HEAD_DIM), lambda b: (layer * n_seq + b, 0)))
    in_specs.append(pl.BlockSpec(memory_space=pl.ANY))
    return pl.pallas_call(
        functools.partial(_attn_sample_kernel, cache_lens=cache_lens),
        grid=(n_seq,),
        in_specs=in_specs,
        out_specs=pl.BlockSpec((n_tok, ATTN_OUT_W), lambda b: (blk0 + b, 0)),
        out_shape=jax.ShapeDtypeStruct(o_a.shape, o_a.dtype),
        input_output_aliases={6: 0},
        compiler_params=_params(("parallel",)),
        name="attn_sample",
    )(qkv, qkv, qkv, *views, o_a)


def _lru_coeffs(xc, wa_ref, wx_ref, ba, bx, lam):
    nb = wa_ref.shape[0]
    bs = wa_ref.shape[1]
    xcb = xc.astype(BF16)
    ra, rx = [], []
    for n in range(nb):
        xs = xcb[:, n * bs:(n + 1) * bs]
        ra.append(jnp.dot(xs, wa_ref[n], preferred_element_type=F32))
        rx.append(jnp.dot(xs, wx_ref[n], preferred_element_type=F32))
    r = jax.nn.sigmoid(jnp.concatenate(ra, axis=1) + ba)
    i = jax.nn.sigmoid(jnp.concatenate(rx, axis=1) + bx)
    softplus_neg_lam = jnp.maximum(-lam, 0.0) + jnp.log1p(jnp.exp(-jnp.abs(lam)))
    log_a = -LRU_C * r * softplus_neg_lam
    a = jnp.exp(log_a)
    b = jnp.sqrt(jnp.tanh(-log_a) * (1.0 + a * a)) * i * xc
    return a, b


def _group_scan(a, b):
    t, c = a.shape
    a3 = a.reshape(t // SUBLANES, SUBLANES, c)
    b3 = b.reshape(t // SUBLANES, SUBLANES, c)
    sub = lax.broadcasted_iota(jnp.int32, a3.shape, 1)
    s = 1
    while s < SUBLANES:
        keep = sub >= s
        a_sh = pltpu.roll(a3, s, axis=1)
        b_sh = pltpu.roll(b3, s, axis=1)
        b3 = jnp.where(keep, a3 * b_sh + b3, b3)
        a3 = jnp.where(keep, a3 * a_sh, a3)
        s *= 2
    return a3.reshape(t, c), b3.reshape(t, c)


def _lru_prompt_kernel(x_ref, cw_ref, cb_ref, wa_ref, wx_ref, ba_ref, bx_ref, lam_ref, o_ref, hlast_ref,
                       tail_ref, a_ref, b_ref, h_ref):
    t, c = x_ref.shape

    @pl.when(pl.program_id(2) == 0)
    def _():
        tail_ref[...] = jnp.zeros_like(tail_ref)
        hlast_ref[...] = jnp.zeros_like(hlast_ref)

    x = x_ref[...].astype(F32)
    tail = tail_ref[...]
    sub = lax.broadcasted_iota(jnp.int32, (SUBLANES, c), 0)
    xc = cb_ref[...] + cw_ref[CONV_W - 1:CONV_W, :] * x
    for k in range(1, CONV_W):
        xr = pltpu.roll(x, k, axis=0)
        first = jnp.where(sub >= k, xr[:SUBLANES], pltpu.roll(tail, k, axis=0))
        xs = jnp.concatenate([first, xr[SUBLANES:]], axis=0)
        xc = xc + cw_ref[CONV_W - 1 - k:CONV_W - k, :] * xs
    tail_ref[...] = x[t - SUBLANES:]

    a, b = _lru_coeffs(xc, wa_ref, wx_ref, ba_ref[...], bx_ref[...], lam_ref[...])
    a, b = _group_scan(a, b)
    a_ref[...] = a
    b_ref[...] = b

    def body(g, h):
        st = pl.multiple_of(g * SUBLANES, SUBLANES)
        hg = a_ref[pl.ds(st, SUBLANES), :] * h + b_ref[pl.ds(st, SUBLANES), :]
        h_ref[pl.ds(st, SUBLANES), :] = hg
        return jnp.broadcast_to(hg[SUBLANES - 1:, :], hg.shape)

    hlast_ref[...] = lax.fori_loop(0, t // SUBLANES, body, hlast_ref[...], unroll=8)
    o_ref[...] = h_ref[...].astype(o_ref.dtype)


def _lru_specs(c_half):
    nb_half = LRU_BLOCKS // 2
    return [
        pl.BlockSpec((CONV_W, c_half), lambda *ids: (0, ids[-2])),
        pl.BlockSpec((1, c_half), lambda *ids: (0, ids[-2])),
        pl.BlockSpec((nb_half, c_half // nb_half, c_half // nb_half), lambda *ids: (ids[-2], 0, 0)),
        pl.BlockSpec((nb_half, c_half // nb_half, c_half // nb_half), lambda *ids: (ids[-2], 0, 0)),
        pl.BlockSpec((1, c_half), lambda *ids: (0, ids[-2])),
        pl.BlockSpec((1, c_half), lambda *ids: (0, ids[-2])),
        pl.BlockSpec((1, c_half), lambda *ids: (0, ids[-2])),
    ]


def _lru_prompt(gates, lru_w, n_seq, seq_len, width):
    t = _row_tile(seq_len, 1024)
    nt = seq_len // t
    ch = width // 2
    return pl.pallas_call(
        _lru_prompt_kernel,
        grid=(n_seq, 2, nt),
        in_specs=[pl.BlockSpec((t, ch), lambda b, hf, i: (b * nt + i, COL_XL * 2 + hf))] + _lru_specs(ch),
        out_specs=[pl.BlockSpec((t, ch), lambda b, hf, i: (b * nt + i, hf)),
                   pl.BlockSpec((SUBLANES, ch), lambda b, hf, i: (b, hf))],
        out_shape=[jax.ShapeDtypeStruct((gates.shape[0], width), BF16),
                   jax.ShapeDtypeStruct((n_seq * SUBLANES, width), F32)],
        scratch_shapes=[pltpu.VMEM((SUBLANES, ch), F32), pltpu.VMEM((t, ch), F32),
                        pltpu.VMEM((t, ch), F32), pltpu.VMEM((t, ch), F32)],
        compiler_params=_params(("parallel", "parallel", "arbitrary")),
        name="lru_prompt",
    )(gates, *lru_w)


def _lru_sample_kernel(x_ref, prev_ref, h0_ref, cw_ref, cb_ref, wa_ref, wx_ref, ba_ref, bx_ref, lam_ref,
                       o_in_ref, o_ref, h_ref):
    del o_in_ref
    t, c = x_ref.shape
    x3 = x_ref[...].astype(F32).reshape(t // SUBLANES, SUBLANES, c)
    p3 = prev_ref[...].reshape(t // SUBLANES, SUBLANES, c)
    sub = lax.broadcasted_iota(jnp.int32, x3.shape, 1)
    xc = cb_ref[...] + cw_ref[CONV_W - 1:CONV_W, :] * x3
    for k in range(1, CONV_W):
        xs = jnp.where(sub >= k, pltpu.roll(x3, k, axis=1), pltpu.roll(p3, k, axis=1))
        xc = xc + cw_ref[CONV_W - 1 - k:CONV_W - k, :] * xs
    xc = xc.reshape(t, c)
    a, b = _lru_coeffs(xc, wa_ref, wx_ref, ba_ref[...], bx_ref[...], lam_ref[...])
    a, b = _group_scan(a, b)
    h = a * h0_ref[...] + b
    h_ref[...] = h
    o_ref[...] = h.astype(o_ref.dtype)


def _lru_sample(gates, prev, h0, lru_w, o_b, row0, n_rows, width):
    t = _row_tile(n_rows, 512)
    ch = width // 2
    blk0 = row0 // t
    return pl.pallas_call(
        _lru_sample_kernel,
        grid=(n_rows // t, 2, 1),
        in_specs=[pl.BlockSpec((t, ch), lambda i, hf, z: (blk0 + i, COL_XL * 2 + hf)),
                  pl.BlockSpec((t, ch), lambda i, hf, z: (i, hf)),
                  pl.BlockSpec((t, ch), lambda i, hf, z: (i, hf))] + _lru_specs(ch)
                 + [pl.BlockSpec(memory_space=pl.ANY)],
        out_specs=[pl.BlockSpec((t, ch), lambda i, hf, z: (blk0 + i, hf)),
                   pl.BlockSpec((t, ch), lambda i, hf, z: (i, hf))],
        out_shape=[jax.ShapeDtypeStruct(o_b.shape, o_b.dtype),
                   jax.ShapeDtypeStruct((n_rows, width), F32)],
        input_output_aliases={10: 0},
        compiler_params=_params(("parallel", "parallel", "arbitrary")),
        name="lru_sample",
    )(gates, prev, h0, *lru_w, o_b)


def _merge_kernel(x_ref, oa_ref, ob_ref, ga_ref, gb_ref, wa_ref, wb_ref, wo_ref, g_ref, b_ref, o_ref, *, alpha):
    ya = jnp.dot(oa_ref[...].astype(BF16), wa_ref[...], preferred_element_type=F32)
    yb = jnp.dot(ob_ref[...], wb_ref[...], preferred_element_type=F32)
    z = jax.nn.sigmoid(ga_ref[...].astype(F32)) * ya + jax.nn.sigmoid(gb_ref[...].astype(F32)) * yb
    m = jnp.dot(z.astype(BF16), wo_ref[...], preferred_element_type=F32)
    o_ref[...] = _layer_norm(alpha * x_ref[...] + m, g_ref[...], b_ref[...])


def _merge(x, o_a, o_b, gates, wa, wb, wo, g, b, alpha):
    m, d = x.shape
    tm = _row_tile(m, 512)
    const = lambda i: (0, 0)
    return pl.pallas_call(
        functools.partial(_merge_kernel, alpha=alpha),
        grid=(m // tm,),
        in_specs=[
            pl.BlockSpec((tm, d), lambda i: (i, 0)),
            pl.BlockSpec((tm, o_a.shape[1]), lambda i: (i, 0)),
            pl.BlockSpec((tm, d), lambda i: (i, 0)),
            pl.BlockSpec((tm, d), lambda i: (i, COL_GA)),
            pl.BlockSpec((tm, d), lambda i: (i, COL_GB)),
            pl.BlockSpec(wa.shape, const),
            pl.BlockSpec(wb.shape, const),
            pl.BlockSpec(wo.shape, const),
            pl.BlockSpec((1, d), const),
            pl.BlockSpec((1, d), const),
        ],
        out_specs=pl.BlockSpec((tm, d), lambda i: (i, 0)),
        out_shape=jax.ShapeDtypeStruct((m, d), F32),
        compiler_params=_params(("parallel",)),
        name="merge_ln",
    )(x, o_a, o_b, gates, gates, wa, wb, wo, g, b)


def kernel(x_prompt, x_sample, cache_kv_w128, cache_kv_w512, cache_kv_w2048, state_rglru_h, state_rglru_conv,
           ln_g, ln_b, ffn1_w_in, ffn1_w_out, ffn2_w_in, ffn2_w_out, w_in, conv_w, conv_b,
           lru_wa, lru_ba, lru_wx, lru_bx, lru_lambda, w_branch_a, w_branch_b, w_out):
    n_p, len_p, d = x_prompt.shape
    n_s, len_s, _ = x_sample.shape
    depth = w_in.shape[0]
    alpha = (2 * depth) ** 0.25
    m_p, m_s = n_p * len_p, n_s * len_s
    caches_in = (cache_kv_w128, cache_kv_w512, cache_kv_w2048)
    past_len = max(c.shape[2] for c in caches_in)
    assert len_s == SUBLANES, "the sample kernels treat one 8-row group as one sequence"
    tm_proj = _row_tile(m_p + m_s, 1024)
    assert len_p % tm_proj == 0 and m_p % tm_proj == 0

    half = HEAD_DIM // 2
    inv = ROPE_THETA ** (-jnp.arange(half, dtype=F32) / half)
    pos = jnp.concatenate([jnp.arange(len_p, dtype=F32),
                           jnp.tile(past_len + jnp.arange(len_s, dtype=F32), tm_proj // len_s)])
    ang = pos[:, None] * inv[None, :]
    cos = jnp.concatenate([jnp.cos(ang), jnp.cos(ang)], axis=1)
    sin = jnp.concatenate([-jnp.sin(ang), jnp.sin(ang)], axis=1)

    attn_cols = 3 * ATTN_W
    cache_rows = tuple(c.reshape(-1, HEAD_DIM) for c in caches_in)
    cache_lens = tuple(c.shape[2] for c in caches_in)
    conv_prev = jnp.pad(state_rglru_conv, ((0, 0), (0, 0), (SUBLANES - (CONV_W - 1), 0), (0, 0)))
    conv_prev = conv_prev.reshape(depth, m_s, d)
    h0_rows = jnp.repeat(state_rglru_h, len_s, axis=1)

    x = jnp.concatenate([x_prompt.reshape(m_p, d), x_sample.reshape(m_s, d)], axis=0)
    outs = {k: [] for k in ("p_kv0", "p_kv1", "p_kv2", "p_h", "p_c", "s_kv0", "s_kv1", "s_kv2", "s_h", "s_c")}
    for l in range(depth):
        g = ln_g[l].reshape(3, 1, d)
        b = ln_b[l].reshape(3, 1, d)
        x = _ffn_ln(x, ffn1_w_in[l].astype(BF16), ffn1_w_out[l].astype(BF16), g[0], b[0], alpha)
        qkv = _proj_qkv(x, w_in[l, :, :attn_cols].astype(BF16), cos, sin, m_p, len_p)
        gates = _proj_gates(x, w_in[l, :, attn_cols:].astype(BF16))
        lru_w = (conv_w[l], conv_b[l].reshape(1, d), lru_wa[l].astype(BF16), lru_wx[l].astype(BF16),
                 lru_ba[l].reshape(1, d), lru_bx[l].reshape(1, d), lru_lambda[l].reshape(1, d))
        o_a = _attn_prompt(qkv, n_p, len_p)
        o_a = _attn_sample(qkv, cache_rows, cache_lens, o_a, l, m_p, n_s, len_s)
        o_b, h_p = _lru_prompt(gates, lru_w, n_p, len_p, d)
        o_b, h_s = _lru_sample(gates, conv_prev[l], h0_rows[l], lru_w, o_b, m_p, m_s, d)
        x = _merge(x, o_a, o_b, gates, w_branch_a[l].astype(BF16), w_branch_b[l].astype(BF16),
                   w_out[l].astype(BF16), g[1], b[1], alpha)
        x = _ffn_ln(x, ffn2_w_in[l].astype(BF16), ffn2_w_out[l].astype(BF16), g[2], b[2], alpha)

        qkv_p = qkv[:m_p].reshape(n_p, len_p, 3 * ATTN_W)
        qkv_s = qkv[m_p:].reshape(n_s, len_s, 3 * ATTN_W)
        for gi, (window, _) in enumerate(ATTN_GROUPS):
            keep = min(window, len_p)
            for name, rows in ((f"p_kv{gi}", qkv_p[:, len_p - keep:]), (f"s_kv{gi}", qkv_s)):
                k_g, v_g = (rows[:, :, c0 + gi * ATTN_OUT_W:c0 + (gi + 1) * ATTN_OUT_W].reshape(
                    rows.shape[0], rows.shape[1], HEADS_PER_GROUP, HEAD_DIM) for c0 in (ATTN_W, 2 * ATTN_W))
                outs[name].append(jnp.stack([k_g, v_g], axis=2))
        tail = CONV_W - 1
        outs["p_h"].append(h_p.reshape(n_p, SUBLANES, d)[:, 0])
        outs["p_c"].append(gates[:m_p].reshape(n_p, len_p, -1)[:, len_p - tail:, :d].astype(F32))
        outs["s_h"].append(h_s.reshape(n_s, len_s, d)[:, -1])
        outs["s_c"].append(gates[m_p:].reshape(n_s, len_s, -1)[:, len_s - tail:, :d].astype(F32))

    st = {k: jnp.stack(v) for k, v in outs.items()}
    return (x[:m_p].reshape(n_p, len_p, d), x[m_p:].reshape(n_s, len_s, d),
            st["p_kv0"], st["p_kv1"], st["p_kv2"], st["p_h"], st["p_c"],
            st["s_kv0"], st["s_kv1"], st["s_kv2"], st["s_h"], st["s_c"])
```

```python
import functools

import jax
import jax.numpy as jnp
from jax import lax
from jax.experimental import pallas as pl
from jax.experimental.pallas import tpu as pltpu

F32 = jnp.float32
BF16 = jnp.bfloat16

HEAD_DIM = 128
HEADS_PER_GROUP = 4
ATTN_GROUPS = ((128, 1), (512, 4), (2048, 16))
N_GROUPS = len(ATTN_GROUPS)
N_HEADS = N_GROUPS * HEADS_PER_GROUP
ATTN_W = N_HEADS * HEAD_DIM
ATTN_OUT_W = HEADS_PER_GROUP * HEAD_DIM
KEYS_PER_BLOCK = 128
LRU_BLOCKS = 8
CONV_W = 4
LRU_C = 8.0
ROPE_THETA = 10000.0
LN_EPS = 1e-5
NEG = -1e30
SUBLANES = 8
VMEM_LIMIT = 56 * 1024 * 1024

PROJ_TN = ATTN_W
HEAD_COL_Q, HEAD_COL_K, HEAD_COL_V = 0, N_HEADS, 2 * N_HEADS
COL_XL, COL_GA, COL_GB = 0, 1, 2


def _row_tile(m, cap):
    t = cap
    while m % t:
        t //= 2
    return t


def _params(sem):
    return pltpu.CompilerParams(dimension_semantics=sem, vmem_limit_bytes=VMEM_LIMIT)


def _layer_norm(y, g, b):
    mu = jnp.mean(y, axis=-1, keepdims=True)
    d = y - mu
    var = jnp.mean(d * d, axis=-1, keepdims=True)
    return d * lax.rsqrt(var + LN_EPS) * g + b


def _nt_dot(a, b):
    return lax.dot_general(a, b, (((1,), (1,)), ((), ())), preferred_element_type=F32)


def _ffn_ln_kernel(x_ref, wg_ref, wu_ref, wo_ref, g_ref, b_ref, *rest, alpha):
    o_ref, xb_ref = rest[-2:]
    j = pl.program_id(1)

    @pl.when(j == 0)
    def _():
        xb_ref[...] = x_ref[...].astype(BF16)

    xb = xb_ref[...]
    hg = jnp.dot(xb, wg_ref[...], preferred_element_type=F32)
    hu = jnp.dot(xb, wu_ref[...], preferred_element_type=F32)
    act = (hg * jax.nn.sigmoid(hg) * hu).astype(BF16)
    part = jnp.dot(act, wo_ref[...], preferred_element_type=F32)

    @pl.when(j == 0)
    def _():
        o_ref[...] = part

    @pl.when(j > 0)
    def _():
        o_ref[...] += part

    @pl.when(j == pl.num_programs(1) - 1)
    def _():
        y = alpha * x_ref[...] + 0.5 * o_ref[...]
        o_ref[...] = _layer_norm(y, g_ref[...], b_ref[...])


def _ffn_ln(x, w_in, w_out, g, b, alpha, *, rows=None, out_rows=None, out_row0=0, out_into=None):
    d = x.shape[1]
    row0, m = rows if rows is not None else (0, x.shape[0])
    out_rows = m if out_rows is None else out_rows
    f = w_out.shape[0]
    tm = _row_tile(m, 1024)
    assert row0 % tm == 0 and out_row0 % tm == 0
    in_blk0, out_blk0 = row0 // tm, out_row0 // tm
    tf = f // 2 if (f // 2) % 128 == 0 else f
    nf = f // tf
    in_specs = [
        pl.BlockSpec((tm, d), lambda i, j: (in_blk0 + i, 0)),
        pl.BlockSpec((d, tf), lambda i, j: (0, j)),
        pl.BlockSpec((d, tf), lambda i, j: (0, j + nf)),
        pl.BlockSpec((tf, d), lambda i, j: (j, 0)),
        pl.BlockSpec((1, d), lambda i, j: (0, 0)),
        pl.BlockSpec((1, d), lambda i, j: (0, 0)),
    ]
    args = [x, w_in, w_in, w_out, g, b]
    aliases = {}
    if out_into is not None:
        assert out_into.shape == (out_rows, d)
        in_specs.append(pl.BlockSpec(memory_space=pl.ANY))
        aliases = {len(args): 0}
        args.append(out_into)
    return pl.pallas_call(
        functools.partial(_ffn_ln_kernel, alpha=alpha),
        grid=(m // tm, nf),
        in_specs=in_specs,
        out_specs=pl.BlockSpec((tm, d), lambda i, j: (out_blk0 + i, 0)),
        out_shape=jax.ShapeDtypeStruct((out_rows, d), F32),
        input_output_aliases=aliases,
        scratch_shapes=[pltpu.VMEM((tm, d), BF16)],
        compiler_params=_params(("parallel", "arbitrary")),
        name="ffn_ln",
    )(*args)


def _proj_gates_kernel(x_ref, w_ref, o_ref, xb_ref):
    @pl.when(pl.program_id(1) == 0)
    def _():
        xb_ref[...] = x_ref[...].astype(BF16)

    o_ref[...] = jnp.dot(xb_ref[...], w_ref[...], preferred_element_type=F32).astype(o_ref.dtype)


def _proj_qkv_kernel(x_ref, w_ref, cos_ref, sin_ref, o_ref, xb_ref, *, q_scale):
    j = pl.program_id(1)

    @pl.when(j == 0)
    def _():
        xb_ref[...] = x_ref[...].astype(BF16)

    is_rope = j < 2
    cos = jnp.where(is_rope, cos_ref[...], 1.0)
    sin = jnp.where(is_rope, sin_ref[...], 0.0)
    scale = jnp.where(j == 0, q_scale, 1.0).astype(F32)
    pair = 2 * HEAD_DIM
    for c in range(PROJ_TN // pair):
        h2 = jnp.dot(xb_ref[...], w_ref[:, c * pair:(c + 1) * pair], preferred_element_type=F32)
        for s in range(2):
            h = h2[:, s * HEAD_DIM:(s + 1) * HEAD_DIM]
            r = pltpu.roll(h, HEAD_DIM // 2, axis=1)
            lo = c * pair + s * HEAD_DIM
            o_ref[:, lo:lo + HEAD_DIM] = (h * cos + r * sin) * scale


def _proj_gates(x, w):
    m, d = x.shape
    n = w.shape[1]
    tm = _row_tile(m, 1024)
    return pl.pallas_call(
        _proj_gates_kernel,
        grid=(m // tm, n // PROJ_TN),
        in_specs=[pl.BlockSpec((tm, d), lambda i, j: (i, 0)),
                  pl.BlockSpec((d, PROJ_TN), lambda i, j: (0, j))],
        out_specs=pl.BlockSpec((tm, PROJ_TN), lambda i, j: (i, j)),
        out_shape=jax.ShapeDtypeStruct((m, n), BF16),
        scratch_shapes=[pltpu.VMEM((tm, d), BF16)],
        compiler_params=_params(("parallel", "arbitrary")),
        name="proj_gates",
    )(x, w)


def _proj_qkv(x, w, cos, sin, prompt_rows, seq_len):
    m, d = x.shape
    n = w.shape[1]
    tm = cos.shape[0] - seq_len
    tiles_per_seq = seq_len // tm
    prompt_tiles = prompt_rows // tm

    def table_map(i, j):
        return (jnp.where(i < prompt_tiles, i % tiles_per_seq, tiles_per_seq), 0)

    return pl.pallas_call(
        functools.partial(_proj_qkv_kernel, q_scale=HEAD_DIM ** -0.5),
        grid=(m // tm, n // PROJ_TN),
        in_specs=[pl.BlockSpec((tm, d), lambda i, j: (i, 0)),
                  pl.BlockSpec((d, PROJ_TN), lambda i, j: (0, j)),
                  pl.BlockSpec((tm, HEAD_DIM), table_map),
                  pl.BlockSpec((tm, HEAD_DIM), table_map)],
        out_specs=pl.BlockSpec((tm, PROJ_TN), lambda i, j: (i, j)),
        out_shape=jax.ShapeDtypeStruct((m, n), F32),
        scratch_shapes=[pltpu.VMEM((tm, d), BF16)],
        compiler_params=_params(("parallel", "arbitrary")),
        name="proj_qkv_rope",
    )(x, w, cos, sin)


def _attn_units(units):
    scores = [_nt_dot(q, k2) + bias for q, k2, _, bias in units]
    stats = []
    for s in scores:
        m = jnp.max(s, axis=-1, keepdims=True)
        p = jnp.exp(s - m)
        stats.append((m, p.astype(BF16), jnp.sum(p, axis=-1, keepdims=True)))
    outs = [jnp.dot(p, v2, preferred_element_type=F32) for (_, p, _), (_, _, v2, _) in zip(stats, units)]
    return [(o / l, m + jnp.log(l)) for o, (m, _, l) in zip(outs, stats)]


def _attn_prompt_kernel(q0, q1, q2, k0, k1, k2, v0, v1, v2, kp0, kp1, kp2, vp0, vp1, vp2,
                        o_ref, og_ref, lse_ref, *, batch):
    has_prev = pl.program_id(1) > 0
    nq = KEYS_PER_BLOCK
    row = lax.broadcasted_iota(jnp.int32, (nq, 2 * nq), 0)
    col = lax.broadcasted_iota(jnp.int32, (nq, 2 * nq), 1)
    bias_in = jnp.where(jnp.logical_and(col >= row, col <= row + nq), 0.0, NEG).astype(F32)
    first_visible = jnp.where(has_prev, row, nq)
    bias_edge = jnp.where(jnp.logical_and(col >= first_visible, col <= row + nq), 0.0, NEG).astype(F32)

    def run(g, dil, refs, starts, from_prev_chunk):
        q_ref, k_ref, v_ref, kprev_ref, vprev_ref = refs

        def rows(ref, start):
            if dil == 1:
                return ref[pl.ds(start, nq), :]
            return ref[pl.ds(start, nq, stride=dil), :]

        units = []
        for q_start, prev_start in starts:
            if from_prev_chunk:
                kp, vp, bias = rows(kprev_ref, prev_start), rows(vprev_ref, prev_start), bias_edge
            else:
                kp, vp, bias = rows(k_ref, prev_start), rows(v_ref, prev_start), bias_in
            units.append((rows(q_ref, q_start).astype(BF16),
                          jnp.concatenate([kp, rows(k_ref, q_start)], axis=0).astype(BF16),
                          jnp.concatenate([vp, rows(v_ref, q_start)], axis=0).astype(BF16), bias))
        for (q_start, _), (o, lse) in zip(starts, _attn_units(units)):
            lse = jnp.broadcast_to(lse, (nq, HEAD_DIM))
            if dil == 1:
                og_ref[g, pl.ds(q_start, nq), :] = o
                lse_ref[g, pl.ds(q_start, nq), :] = lse
            else:
                og_ref[g, pl.ds(q_start, nq, stride=dil), :] = o
                lse_ref[g, pl.ds(q_start, nq, stride=dil), :] = lse

    def batch_of(n):
        return max(u for u in range(1, batch + 1) if n % u == 0)

    chunk = q0.shape[0]
    all_refs = ((q0, k0, v0, kp0, vp0), (q1, k1, v1, kp1, vp1), (q2, k2, v2, kp2, vp2))
    for g, (window, dil) in enumerate(ATTN_GROUPS):
        span = nq * dil
        refs = all_refs[g]

        u1 = batch_of(dil)

        def first(it, carry, g=g, dil=dil, refs=refs, u1=u1):
            run(g, dil, refs, [(it * u1 + u, it * u1 + u) for u in range(u1)], True)
            return carry
        lax.fori_loop(0, dil // u1, first, 0)
        n_rest = (chunk // span - 1) * dil
        if n_rest:
            u2 = batch_of(n_rest)

            def rest(it, carry, g=g, dil=dil, span=span, refs=refs, u2=u2):
                starts = []
                for u in range(u2):
                    idx = it * u2 + u
                    q_start = (idx // dil + 1) * span + idx % dil
                    if dil == 1:
                        q_start = pl.multiple_of(q_start, nq)
                    starts.append((q_start, q_start - span))
                run(g, dil, refs, starts, False)
                return carry
            lax.fori_loop(0, n_rest // u2, rest, 0)

    l0, l1, l2 = lse_ref[0], lse_ref[1], lse_ref[2]
    mx = jnp.maximum(jnp.maximum(l0, l1), l2)
    w0, w1, w2 = jnp.exp(l0 - mx), jnp.exp(l1 - mx), jnp.exp(l2 - mx)
    o_ref[...] = (w0 * og_ref[0] + w1 * og_ref[1] + w2 * og_ref[2]) / (w0 + w1 + w2)


def _attn_prompt(qkv, n_seq, seq_len):
    chunk = KEYS_PER_BLOCK * max(d for _, d in ATTN_GROUPS)
    assert seq_len % chunk == 0
    nc = seq_len // chunk
    in_specs = []
    for head_col in (HEAD_COL_Q, HEAD_COL_K, HEAD_COL_V):
        for g in range(N_GROUPS):
            in_specs.append(pl.BlockSpec(
                (chunk, HEAD_DIM),
                lambda b, c, h, hc=head_col, g=g: (b * nc + c, hc + g * HEADS_PER_GROUP + h)))
    for head_col in (HEAD_COL_K, HEAD_COL_V):
        for g, (window, dil) in enumerate(ATTN_GROUPS):
            span = KEYS_PER_BLOCK * dil
            per_chunk = chunk // span
            in_specs.append(pl.BlockSpec(
                (span, HEAD_DIM),
                lambda b, c, h, hc=head_col, g=g, pc=per_chunk:
                (b * nc * pc + jnp.maximum(c * pc - 1, 0), hc + g * HEADS_PER_GROUP + h)))
    return pl.pallas_call(
        functools.partial(_attn_prompt_kernel, batch=8),
        grid=(n_seq, nc, HEADS_PER_GROUP),
        in_specs=in_specs,
        out_specs=pl.BlockSpec((chunk, HEAD_DIM), lambda b, c, h: (b * nc + c, h)),
        out_shape=jax.ShapeDtypeStruct((qkv.shape[0], ATTN_OUT_W), F32),
        scratch_shapes=[pltpu.VMEM((N_GROUPS, chunk, HEAD_DIM), F32),
                        pltpu.VMEM((N_GROUPS, chunk, HEAD_DIM), F32)],
        compiler_params=_params(("parallel", "parallel", "parallel")),
        name="attn_prompt",
    )(*([qkv] * 15))


def _attn_sample_kernel(q_ref, kn_ref, vn_ref, c0_ref, c1_ref, c2_ref, o_in_ref, o_ref, *, cache_lens, nt):
    del o_in_ref
    n_seq = q_ref.shape[0] // nt
    nh = HEADS_PER_GROUP
    kv_rows = 2 * nh
    caches = (c0_ref, c1_ref, c2_ref)
    zpad = jnp.zeros((KEYS_PER_BLOCK - nt, HEAD_DIM), BF16)
    units = [(s, h) for s in range(n_seq) for h in range(nh)]

    kvs, vns, s_cache, s_new = {}, {}, [], []
    for g in range(N_GROUPS):
        lc = cache_lens[g]
        sc, sn = [], []
        for s, h in units:
            col = (g * nh + h) * HEAD_DIM
            tok = slice(s * nt, (s + 1) * nt)
            q = q_ref[tok, col:col + HEAD_DIM].astype(BF16)
            if len(caches[g].shape) == 2:
                kv = caches[g][pl.ds(s * lc * kv_rows + h, 2 * lc, stride=nh), :]
            else:
                periods = caches[g].shape[0] // n_seq
                kv = caches[g][s * periods:(s + 1) * periods, pl.ds(h, 2 * nt, stride=nh), :]
                kv = kv.reshape(kv.shape[0] * kv.shape[1], HEAD_DIM)
            kvs[g, s, h] = kv.astype(BF16)
            kn = jnp.concatenate([kn_ref[tok, col:col + HEAD_DIM].astype(BF16), zpad], axis=0)
            vns[g, s, h] = jnp.concatenate([vn_ref[tok, col:col + HEAD_DIM].astype(BF16), zpad], axis=0)
            sc.append(_nt_dot(q, kvs[g, s, h]))
            sn.append(_nt_dot(q, kn))
        s_cache.append(jnp.concatenate(sc, axis=0))
        s_new.append(jnp.concatenate(sn, axis=0))

    rows = len(units) * nt
    i_new = jnp.bitwise_and(lax.broadcasted_iota(jnp.int32, (rows, KEYS_PER_BLOCK), 0), nt - 1)
    d_new = i_new - lax.broadcasted_iota(jnp.int32, (rows, KEYS_PER_BLOCK), 1)
    ms, ls, p_vs, p_ns = [], [], [], []
    for g, (window, dil) in enumerate(ATTN_GROUPS):
        lc = cache_lens[g]
        n_kv = s_cache[g].shape[1]
        i_c = jnp.bitwise_and(lax.broadcasted_iota(jnp.int32, (rows, n_kv), 0), nt - 1)
        c_c = lax.broadcasted_iota(jnp.int32, (rows, n_kv), 1)
        if len(caches[g].shape) == 2:
            pos = jnp.right_shift(c_c, 1)
        else:
            shift = (2 * nt).bit_length() - 1
            pos = dil * jnp.right_shift(c_c, shift) + jnp.bitwise_and(jnp.right_shift(c_c, 1), nt - 1)
        d_c = lc + i_c - pos
        ok_c = jnp.logical_and(jnp.bitwise_and(c_c, 1) == 0,
                               jnp.logical_and(jnp.bitwise_and(d_c, dil - 1) == 0, d_c <= window))
        ok_n = jnp.logical_and(jnp.logical_and(d_new >= 0, jnp.bitwise_and(d_new, dil - 1) == 0),
                               d_new <= window)
        s_c = jnp.where(ok_c, s_cache[g], NEG)
        s_n = jnp.where(ok_n, s_new[g], NEG)
        m = jnp.maximum(jnp.max(s_c, axis=-1, keepdims=True), jnp.max(s_n, axis=-1, keepdims=True))
        p_c = jnp.exp(s_c - m)
        p_n = jnp.exp(s_n - m)
        ms.append(m)
        ls.append(jnp.sum(p_c, axis=-1, keepdims=True) + jnp.sum(p_n, axis=-1, keepdims=True))
        p_vs.append(pltpu.roll(p_c, 1, axis=1))
        p_ns.append(p_n)

    accs = []
    for g in range(N_GROUPS):
        accs.append(jnp.concatenate(
            [jnp.dot(p_vs[g][u * nt:(u + 1) * nt].astype(BF16), kvs[g, s, h], preferred_element_type=F32)
             + jnp.dot(p_ns[g][u * nt:(u + 1) * nt].astype(BF16), vns[g, s, h], preferred_element_type=F32)
             for u, (s, h) in enumerate(units)], axis=0))

    mx = jnp.maximum(jnp.maximum(ms[0], ms[1]), ms[2])
    es = [jnp.exp(m - mx) for m in ms]
    num = es[0] * accs[0] + es[1] * accs[1] + es[2] * accs[2]
    den = es[0] * ls[0] + es[1] * ls[1] + es[2] * ls[2]
    out = num / den
    for u, (s, h) in enumerate(units):
        o_ref[s * nt:(s + 1) * nt, h * HEAD_DIM:(h + 1) * HEAD_DIM] = out[u * nt:(u + 1) * nt]


def _attn_sample(qkv, caches, cache_lens, o_a, layer, row0, n_seq, n_tok):
    kv_rows = 2 * HEADS_PER_GROUP
    per_step = 2 if n_seq % 2 == 0 and row0 % (2 * n_tok) == 0 else 1
    rows = per_step * n_tok
    blk0 = row0 // rows
    seq0 = layer * n_seq // per_step
    in_specs = [pl.BlockSpec((rows, ATTN_W), lambda b, cb=cb: (blk0 + b, cb)) for cb in range(3)]
    views = []
    for g, (window, dil) in enumerate(ATTN_GROUPS):
        lc = cache_lens[g]
        if dil > n_tok and lc % dil == 0:
            periods = lc // dil
            views.append(caches[g].reshape(-1, dil * kv_rows, HEAD_DIM))
            in_specs.append(pl.BlockSpec((per_step * periods, n_tok * kv_rows, HEAD_DIM),
                                         lambda b: (seq0 + b, 0, 0)))
        else:
            views.append(caches[g].reshape(-1, HEAD_DIM))
            in_specs.append(pl.BlockSpec((per_step * lc * kv_rows, HEAD_DIM), lambda b: (seq0 + b, 0)))
    in_specs.append(pl.BlockSpec(memory_space=pl.ANY))
    return pl.pallas_call(
        functools.partial(_attn_sample_kernel, cache_lens=cache_lens, nt=n_tok),
        grid=(n_seq // per_step,),
        in_specs=in_specs,
        out_specs=pl.BlockSpec((rows, ATTN_OUT_W), lambda b: (blk0 + b, 0)),
        out_shape=jax.ShapeDtypeStruct(o_a.shape, o_a.dtype),
        input_output_aliases={6: 0},
        compiler_params=_params(("parallel",)),
        name="attn_sample",
    )(qkv, qkv, qkv, *views, o_a)


def _lru_coeffs(xc, wa_ref, wx_ref, ba, bx, lam):
    nb = wa_ref.shape[0]
    bs = wa_ref.shape[1]
    xcb = xc.astype(BF16)
    ra, rx = [], []
    for n in range(nb):
        xs = xcb[:, n * bs:(n + 1) * bs]
        ra.append(jnp.dot(xs, wa_ref[n], preferred_element_type=F32))
        rx.append(jnp.dot(xs, wx_ref[n], preferred_element_type=F32))
    r = jax.nn.sigmoid(jnp.concatenate(ra, axis=1) + ba)
    i = jax.nn.sigmoid(jnp.concatenate(rx, axis=1) + bx)
    softplus_neg_lam = jnp.maximum(-lam, 0.0) + jnp.log1p(jnp.exp(-jnp.abs(lam)))
    log_a = -LRU_C * r * softplus_neg_lam
    a = jnp.exp(log_a)
    b = jnp.sqrt(jnp.tanh(-log_a) * (1.0 + a * a)) * i * xc
    return a, b


def _group_scan(a, b):
    t, c = a.shape
    a3 = a.reshape(t // SUBLANES, SUBLANES, c)
    b3 = b.reshape(t // SUBLANES, SUBLANES, c)
    sub = lax.broadcasted_iota(jnp.int32, a3.shape, 1)
    s = 1
    while s < SUBLANES:
        keep = sub >= s
        a_sh = pltpu.roll(a3, s, axis=1)
        b_sh = pltpu.roll(b3, s, axis=1)
        b3 = jnp.where(keep, a3 * b_sh + b3, b3)
        a3 = jnp.where(keep, a3 * a_sh, a3)
        s *= 2
    return a3.reshape(t, c), b3.reshape(t, c)


def _lru_prompt_kernel(x_ref, cw_ref, cb_ref, wa_ref, wx_ref, ba_ref, bx_ref, lam_ref, o_ref, hlast_ref,
                       tail_ref, a_ref, b_ref, h_ref):
    t, c = x_ref.shape

    @pl.when(pl.program_id(2) == 0)
    def _():
        tail_ref[...] = jnp.zeros_like(tail_ref)
        hlast_ref[...] = jnp.zeros_like(hlast_ref)

    x = x_ref[...].astype(F32)
    tail = tail_ref[...]
    sub = lax.broadcasted_iota(jnp.int32, (SUBLANES, c), 0)
    xc = cb_ref[...] + cw_ref[CONV_W - 1:CONV_W, :] * x
    for k in range(1, CONV_W):
        xr = pltpu.roll(x, k, axis=0)
        first = jnp.where(sub >= k, xr[:SUBLANES], pltpu.roll(tail, k, axis=0))
        xs = jnp.concatenate([first, xr[SUBLANES:]], axis=0)
        xc = xc + cw_ref[CONV_W - 1 - k:CONV_W - k, :] * xs
    tail_ref[...] = x[t - SUBLANES:]

    a, b = _lru_coeffs(xc, wa_ref, wx_ref, ba_ref[...], bx_ref[...], lam_ref[...])
    a, b = _group_scan(a, b)
    a_ref[...] = a
    b_ref[...] = b

    def body(g, h):
        st = pl.multiple_of(g * SUBLANES, SUBLANES)
        hg = a_ref[pl.ds(st, SUBLANES), :] * h + b_ref[pl.ds(st, SUBLANES), :]
        h_ref[pl.ds(st, SUBLANES), :] = hg
        return jnp.broadcast_to(hg[SUBLANES - 1:, :], hg.shape)

    hlast_ref[...] = lax.fori_loop(0, t // SUBLANES, body, hlast_ref[...], unroll=8)
    o_ref[...] = h_ref[...].astype(o_ref.dtype)


def _lru_specs(c_half):
    nb_half = LRU_BLOCKS // 2
    return [
        pl.BlockSpec((CONV_W, c_half), lambda *ids: (0, ids[-2])),
        pl.BlockSpec((1, c_half), lambda *ids: (0, ids[-2])),
        pl.BlockSpec((nb_half, c_half // nb_half, c_half // nb_half), lambda *ids: (ids[-2], 0, 0)),
        pl.BlockSpec((nb_half, c_half // nb_half, c_half // nb_half), lambda *ids: (ids[-2], 0, 0)),
        pl.BlockSpec((1, c_half), lambda *ids: (0, ids[-2])),
        pl.BlockSpec((1, c_half), lambda *ids: (0, ids[-2])),
        pl.BlockSpec((1, c_half), lambda *ids: (0, ids[-2])),
    ]


def _lru_prompt(gates, lru_w, n_seq, seq_len, width):
    t = _row_tile(seq_len, 1024)
    nt = seq_len // t
    ch = width // 2
    return pl.pallas_call(
        _lru_prompt_kernel,
        grid=(n_seq, 2, nt),
        in_specs=[pl.BlockSpec((t, ch), lambda b, hf, i: (b * nt + i, COL_XL * 2 + hf))] + _lru_specs(ch),
        out_specs=[pl.BlockSpec((t, ch), lambda b, hf, i: (b * nt + i, hf)),
                   pl.BlockSpec((SUBLANES, ch), lambda b, hf, i: (b, hf))],
        out_shape=[jax.ShapeDtypeStruct((gates.shape[0], width), BF16),
                   jax.ShapeDtypeStruct((n_seq * SUBLANES, width), F32)],
        scratch_shapes=[pltpu.VMEM((SUBLANES, ch), F32), pltpu.VMEM((t, ch), F32),
                        pltpu.VMEM((t, ch), F32), pltpu.VMEM((t, ch), F32)],
        compiler_params=_params(("parallel", "parallel", "arbitrary")),
        name="lru_prompt",
    )(gates, *lru_w)


def _lru_sample_kernel(x_ref, prev_ref, h0_ref, cw_ref, cb_ref, wa_ref, wx_ref, ba_ref, bx_ref, lam_ref,
                       o_in_ref, o_ref, h_ref):
    del o_in_ref
    t, c = x_ref.shape
    x3 = x_ref[...].astype(F32).reshape(t // SUBLANES, SUBLANES, c)
    p3 = prev_ref[...].reshape(t // SUBLANES, SUBLANES, c)
    sub = lax.broadcasted_iota(jnp.int32, x3.shape, 1)
    xc = cb_ref[...] + cw_ref[CONV_W - 1:CONV_W, :] * x3
    for k in range(1, CONV_W):
        xs = jnp.where(sub >= k, pltpu.roll(x3, k, axis=1), pltpu.roll(p3, k, axis=1))
        xc = xc + cw_ref[CONV_W - 1 - k:CONV_W - k, :] * xs
    xc = xc.reshape(t, c)
    a, b = _lru_coeffs(xc, wa_ref, wx_ref, ba_ref[...], bx_ref[...], lam_ref[...])
    a, b = _group_scan(a, b)
    h = a * h0_ref[...] + b
    h_ref[...] = h
    o_ref[...] = h.astype(o_ref.dtype)


def _lru_sample(gates, prev, h0, lru_w, o_b, row0, n_rows, width):
    t = _row_tile(n_rows, 512)
    ch = width // 2
    blk0 = row0 // t
    return pl.pallas_call(
        _lru_sample_kernel,
        grid=(n_rows // t, 2, 1),
        in_specs=[pl.BlockSpec((t, ch), lambda i, hf, z: (blk0 + i, COL_XL * 2 + hf)),
                  pl.BlockSpec((t, ch), lambda i, hf, z: (i, hf)),
                  pl.BlockSpec((t, ch), lambda i, hf, z: (i, hf))] + _lru_specs(ch)
                 + [pl.BlockSpec(memory_space=pl.ANY)],
        out_specs=[pl.BlockSpec((t, ch), lambda i, hf, z: (blk0 + i, hf)),
                   pl.BlockSpec((t, ch), lambda i, hf, z: (i, hf))],
        out_shape=[jax.ShapeDtypeStruct(o_b.shape, o_b.dtype),
                   jax.ShapeDtypeStruct((n_rows, width), F32)],
        input_output_aliases={10: 0},
        compiler_params=_params(("parallel", "parallel", "arbitrary")),
        name="lru_sample",
    )(gates, prev, h0, *lru_w, o_b)


def _merge_kernel(x_ref, oa_ref, ob_ref, ga_ref, gb_ref, wa_ref, wb_ref, wo_ref, g_ref, b_ref, o_ref, *, alpha):
    ya = jnp.dot(oa_ref[...].astype(BF16), wa_ref[...], preferred_element_type=F32)
    yb = jnp.dot(ob_ref[...], wb_ref[...], preferred_element_type=F32)
    z = jax.nn.sigmoid(ga_ref[...].astype(F32)) * ya + jax.nn.sigmoid(gb_ref[...].astype(F32)) * yb
    m = jnp.dot(z.astype(BF16), wo_ref[...], preferred_element_type=F32)
    o_ref[...] = _layer_norm(alpha * x_ref[...] + m, g_ref[...], b_ref[...])


def _merge(x, o_a, o_b, gates, wa, wb, wo, g, b, alpha):
    m, d = x.shape
    tm = _row_tile(m, 512)
    const = lambda i: (0, 0)
    return pl.pallas_call(
        functools.partial(_merge_kernel, alpha=alpha),
        grid=(m // tm,),
        in_specs=[
            pl.BlockSpec((tm, d), lambda i: (i, 0)),
            pl.BlockSpec((tm, o_a.shape[1]), lambda i: (i, 0)),
            pl.BlockSpec((tm, d), lambda i: (i, 0)),
            pl.BlockSpec((tm, d), lambda i: (i, COL_GA)),
            pl.BlockSpec((tm, d), lambda i: (i, COL_GB)),
            pl.BlockSpec(wa.shape, const),
            pl.BlockSpec(wb.shape, const),
            pl.BlockSpec(wo.shape, const),
            pl.BlockSpec((1, d), const),
            pl.BlockSpec((1, d), const),
        ],
        out_specs=pl.BlockSpec((tm, d), lambda i: (i, 0)),
        out_shape=jax.ShapeDtypeStruct((m, d), F32),
        compiler_params=_params(("parallel",)),
        name="merge_ln",
    )(x, o_a, o_b, gates, gates, wa, wb, wo, g, b)


def kernel(x_prompt, x_sample, cache_kv_w128, cache_kv_w512, cache_kv_w2048, state_rglru_h, state_rglru_conv,
           ln_g, ln_b, ffn1_w_in, ffn1_w_out, ffn2_w_in, ffn2_w_out, w_in, conv_w, conv_b,
           lru_wa, lru_ba, lru_wx, lru_bx, lru_lambda, w_branch_a, w_branch_b, w_out):
    n_p, len_p, d = x_prompt.shape
    n_s, len_s, _ = x_sample.shape
    depth = w_in.shape[0]
    alpha = (2 * depth) ** 0.25
    m_p, m_s = n_p * len_p, n_s * len_s
    caches_in = (cache_kv_w128, cache_kv_w512, cache_kv_w2048)
    past_len = max(c.shape[2] for c in caches_in)
    assert len_s == SUBLANES, "the sample kernels treat one 8-row group as one sequence"
    tm_proj = _row_tile(m_p + m_s, 1024)
    assert len_p % tm_proj == 0 and m_p % tm_proj == 0

    half = HEAD_DIM // 2
    inv = ROPE_THETA ** (-jnp.arange(half, dtype=F32) / half)
    pos = jnp.concatenate([jnp.arange(len_p, dtype=F32),
                           jnp.tile(past_len + jnp.arange(len_s, dtype=F32), tm_proj // len_s)])
    ang = pos[:, None] * inv[None, :]
    cos = jnp.concatenate([jnp.cos(ang), jnp.cos(ang)], axis=1)
    sin = jnp.concatenate([-jnp.sin(ang), jnp.sin(ang)], axis=1)

    attn_cols = 3 * ATTN_W
    cache_rows = tuple(c.reshape(-1, HEAD_DIM) for c in caches_in)
    cache_lens = tuple(c.shape[2] for c in caches_in)
    conv_prev = jnp.pad(state_rglru_conv, ((0, 0), (0, 0), (SUBLANES - (CONV_W - 1), 0), (0, 0)))
    conv_prev = conv_prev.reshape(depth, m_s, d)
    h0_rows = jnp.repeat(state_rglru_h, len_s, axis=1)

    x = y_p = y_s = None
    outs = {k: [] for k in ("p_kv0", "p_kv1", "p_kv2", "p_h", "p_c", "s_kv0", "s_kv1", "s_kv2", "s_h", "s_c")}
    for l in range(depth):
        g = ln_g[l].reshape(3, 1, d)
        b = ln_b[l].reshape(3, 1, d)
        ffn1 = (ffn1_w_in[l].astype(BF16), ffn1_w_out[l].astype(BF16), g[0], b[0], alpha)
        ffn2 = (ffn2_w_in[l].astype(BF16), ffn2_w_out[l].astype(BF16), g[2], b[2], alpha)
        if l == 0:
            x = _ffn_ln(x_prompt.reshape(m_p, d), *ffn1, out_rows=m_p + m_s)
            x = _ffn_ln(x_sample.reshape(m_s, d), *ffn1, out_rows=m_p + m_s, out_row0=m_p, out_into=x)
        else:
            x = _ffn_ln(x, *ffn1)
        qkv =_proj_qkv(x, w_in[l, :, :attn_cols].astype(BF16), cos, sin, m_p, len_p)
        gates = _proj_gates(x, w_in[l, :, attn_cols:].astype(BF16))
        lru_w = (conv_w[l], conv_b[l].reshape(1, d), lru_wa[l].astype(BF16), lru_wx[l].astype(BF16),
                 lru_ba[l].reshape(1, d), lru_bx[l].reshape(1, d), lru_lambda[l].reshape(1, d))
        o_a = _attn_prompt(qkv, n_p, len_p)
        o_a = _attn_sample(qkv, cache_rows, cache_lens, o_a, l, m_p, n_s, len_s)
        o_b, h_p = _lru_prompt(gates, lru_w, n_p, len_p, d)
        o_b, h_s = _lru_sample(gates, conv_prev[l], h0_rows[l], lru_w, o_b, m_p, m_s, d)
        x = _merge(x, o_a, o_b, gates, w_branch_a[l].astype(BF16), w_branch_b[l].astype(BF16),
                   w_out[l].astype(BF16), g[1], b[1], alpha)
        if l == depth - 1:
            y_p = _ffn_ln(x, *ffn2, rows=(0, m_p))
            y_s = _ffn_ln(x, *ffn2, rows=(m_p, m_s))
        else:
            x = _ffn_ln(x, *ffn2)

        def last_rows(arr, n_rows, c0, c1):
            return jnp.stack([arr[(s + 1) * len_p - n_rows:(s + 1) * len_p, c0:c1] for s in range(n_p)])

        for gi, (window, _) in enumerate(ATTN_GROUPS):
            keep = min(window, len_p)
            kv_p, kv_s = [], []
            for c0 in (ATTN_W + gi * ATTN_OUT_W, 2 * ATTN_W + gi * ATTN_OUT_W):
                kv_p.append(last_rows(qkv, keep, c0, c0 + ATTN_OUT_W).reshape(
                    n_p, keep, HEADS_PER_GROUP, HEAD_DIM))
                kv_s.append(qkv[m_p:, c0:c0 + ATTN_OUT_W].reshape(n_s, len_s, HEADS_PER_GROUP, HEAD_DIM))
            outs[f"p_kv{gi}"].append(jnp.stack(kv_p, axis=2))
            outs[f"s_kv{gi}"].append(jnp.stack(kv_s, axis=2))
        tail = CONV_W - 1
        outs["p_h"].append(h_p.reshape(n_p, SUBLANES, d)[:, 0])
        outs["p_c"].append(last_rows(gates, tail, 0, d).astype(F32))
        outs["s_h"].append(h_s.reshape(n_s, len_s, d)[:, -1])
        outs["s_c"].append(gates[m_p:, :d].reshape(n_s, len_s, d)[:, len_s - tail:].astype(F32))

    st = {k: jnp.stack(v) for k, v in outs.items()}
    return (y_p.reshape(n_p, len_p, d), y_s.reshape(n_s, len_s, d),
            st["p_kv0"], st["p_kv1"], st["p_kv2"], st["p_h"], st["p_c"],
            st["s_kv0"], st["s_kv1"], st["s_kv2"], st["s_h"], st["s_c"])
```
